```python
import math
import jax, jax.numpy as jnp
from jax import lax
import numpy as np


D_MODEL = 1024
BATCH = 8
SEQ = 4096
DEPTH = 2

D_FF = 2816
EPS = 1e-6

A_GROUPS = 4
A_DIM = 128
A_CHUNK = 128
A_WIDTH = A_GROUPS * A_DIM

B_HEADS = 4
B_DK = 128
B_DV = 128
B_CONV = 4
B_CHUNK = 64
B_CONV_CH = 2 * B_HEADS * B_DK + B_HEADS * B_DV

C_HEADS = 8
C_DQK = 64
C_DV = 128
C_CHUNK = 64
GATE_CAP = 15.0

N_EVEN = (DEPTH + 1) // 2
N_ODD = DEPTH // 2

EVEN_SPLITS = (A_WIDTH, A_WIDTH, B_HEADS * B_DK, B_HEADS * B_DK, B_HEADS * B_DV, B_HEADS, B_HEADS, B_HEADS * B_DV)
EVEN_IN = 2 * A_WIDTH + 2 * B_HEADS * B_DK + 2 * B_HEADS * B_DV + 2 * B_HEADS
EVEN_OUT = A_WIDTH + B_HEADS * B_DV
ODD_SPLITS = (C_HEADS * C_DQK, C_HEADS * C_DQK, C_HEADS * C_DV, C_HEADS, C_HEADS, C_HEADS * C_DV)
ODD_IN = 2 * C_HEADS * C_DQK + 2 * C_HEADS * C_DV + 2 * C_HEADS
ODD_OUT = C_HEADS * C_DV

kernel_name = "hybrid_gmlp_gdn_mlstm_macaron"


def _split(z, sizes):
    out, off = [], 0
    for s in sizes:
        out.append(z[..., off:off + s])
        off += s
    return out


def rms_norm(x, g):
    xf = x.astype(jnp.float32)
    y = xf * lax.rsqrt(jnp.mean(xf * xf, axis=-1, keepdims=True) + EPS)
    return (y * g.astype(jnp.float32)).astype(x.dtype)


def swiglu(x, w_gu, w_down):
    g, u = jnp.split(x @ w_gu, 2, axis=-1)
    return (jax.nn.silu(g) * u) @ w_down


def _to_chunks(t, L):
    b, s = t.shape[:2]
    t = t.astype(jnp.float32).reshape(b, s // L, L, *t.shape[2:])
    return jnp.moveaxis(t, 2, 3)


def _from_chunks(o):
    n, b, h, L, d = o.shape
    return o.transpose(1, 0, 3, 2, 4).reshape(b, n * L, h, d)


def spatial_gating(u, v, ln_g, w_s, b_s):
    b, s, _ = v.shape
    n = s // A_CHUNK
    vf = v.astype(jnp.float32).reshape(b, s, A_GROUPS, A_DIM)
    mu = jnp.mean(vf, axis=-1, keepdims=True)
    var = jnp.mean(jnp.square(vf - mu), axis=-1, keepdims=True)
    vn = (vf - mu) * lax.rsqrt(var + EPS) * ln_g.astype(jnp.float32).reshape(A_GROUPS, A_DIM)
    vn = vn.astype(u.dtype).reshape(b, n, A_CHUNK, A_GROUPS, A_DIM)
    causal = jnp.tril(jnp.ones((A_CHUNK, A_CHUNK), dtype=bool))
    w = jnp.where(causal[None], w_s, jnp.zeros((), w_s.dtype))
    mixed = jnp.einsum('gts,bnsgc->bntgc', w, vn) + b_s.T[None, None, :, :, None]
    return u * mixed.reshape(b, s, A_WIDTH).astype(u.dtype)


def causal_dwconv(x, w):
    K, C = w.shape
    return lax.conv_general_dilated(x, w[:, None, :].astype(x.dtype), window_strides=(1,),
                                    padding=[(K - 1, 0)], dimension_numbers=('NWC', 'WIO', 'NWC'),
                                    feature_group_count=C)


def l2norm(t):
    tf = t.astype(jnp.float32)
    return tf * lax.rsqrt(jnp.sum(tf * tf, axis=-1, keepdims=True) + EPS)


def gated_delta_rule(q, k, v, g, beta):
    L = B_CHUNK
    dk, dv = q.shape[-1], v.shape[-1]
    b, h = q.shape[0], q.shape[2]
    q = _to_chunks(q, L) * (dk ** -0.5)
    k = _to_chunks(k, L)
    v = _to_chunks(v, L)
    beta = _to_chunks(beta, L)
    gc = jnp.cumsum(_to_chunks(g, L), axis=-1)
    idx = jnp.arange(L)
    causal = idx[:, None] >= idx[None, :]
    strict = idx[:, None] > idx[None, :]
    decay = jnp.exp(jnp.where(causal, gc[..., :, None] - gc[..., None, :], -jnp.inf))
    kb = k * beta[..., None]
    low = jnp.where(strict, jnp.einsum('bnhid,bnhjd->bnhij', kb, k) * decay, 0.0)
    a_mat = low + jnp.eye(L, dtype=jnp.float32)
    rhs = jnp.concatenate([v * beta[..., None], kb * jnp.exp(gc)[..., None]], axis=-1)
    sol = lax.linalg.triangular_solve(a_mat, rhs, left_side=True, lower=True, unit_diagonal=True)
    u_val, w_kd = sol[..., :dv], sol[..., dv:]
    attn = jnp.where(causal, jnp.einsum('bnhid,bnhjd->bnhij', q, k) * decay, 0.0)

    def step(S, inp):
        q_i, k_i, u_i, w_i, g_i, a_i = inp
        v_new = u_i - jnp.einsum('bhld,bhde->bhle', w_i, S)
        o = (jnp.einsum('bhld,bhde->bhle', q_i * jnp.exp(g_i)[..., None], S)
             + jnp.einsum('bhij,bhje->bhie', a_i, v_new))
        g_last = g_i[..., -1]
        k_dec = k_i * jnp.exp(g_last[..., None] - g_i)[..., None]
        S = S * jnp.exp(g_last)[..., None, None] + jnp.einsum('bhld,bhle->bhde', k_dec, v_new)
        return S, o

    xs = tuple(jnp.moveaxis(t, 1, 0) for t in (q, k, u_val, w_kd, gc, attn))
    S0 = jnp.zeros((b, h, dk, dv), jnp.float32)
    _, o = lax.scan(step, S0, xs)
    return _from_chunks(o)


def even_mixer(h, w_in, w_out, a_ln_g, a_ws, a_bs, b_conv, b_a_log, b_dt_bias, b_onorm_g):
    bsz, s, _ = h.shape
    z = h @ w_in
    a_u, a_v, b_q, b_k, b_v, b_beta, b_a, b_gate = _split(z, EVEN_SPLITS)
    y_a = spatial_gating(jax.nn.gelu(a_u), jax.nn.gelu(a_v), a_ln_g, a_ws, a_bs)
    qkv = jax.nn.silu(causal_dwconv(jnp.concatenate([b_q, b_k, b_v], axis=-1), b_conv))
    q, k, v = _split(qkv, (B_HEADS * B_DK, B_HEADS * B_DK, B_HEADS * B_DV))
    q = l2norm(q.reshape(bsz, s, B_HEADS, B_DK))
    k = l2norm(k.reshape(bsz, s, B_HEADS, B_DK))
    v = v.reshape(bsz, s, B_HEADS, B_DV)
    beta = jax.nn.sigmoid(b_beta.astype(jnp.float32))
    g = -jnp.exp(b_a_log.astype(jnp.float32)) * jax.nn.softplus(b_a.astype(jnp.float32) + b_dt_bias.astype(jnp.float32))
    o = gated_delta_rule(q, k, v, g, beta)
    o = rms_norm(o, b_onorm_g) * jax.nn.silu(b_gate.astype(jnp.float32).reshape(bsz, s, B_HEADS, B_DV))
    y_b = o.reshape(bsz, s, B_HEADS * B_DV).astype(h.dtype)
    return jnp.concatenate([y_a, y_b], axis=-1) @ w_out


def mlstm(q, k, v, li, lf):
    L = C_CHUNK
    dqk, dv = q.shape[-1], v.shape[-1]
    b, h = q.shape[0], q.shape[2]
    q = _to_chunks(q, L)
    k = _to_chunks(k, L) * (dqk ** -0.5)
    v = _to_chunks(v, L)
    li = _to_chunks(li, L)
    bcum = jnp.cumsum(_to_chunks(lf, L), axis=-1)
    idx = jnp.arange(L)
    causal = idx[:, None] >= idx[None, :]
    dlog = jnp.where(causal, bcum[..., :, None] - bcum[..., None, :] + li[..., None, :], -jnp.inf)
    dmax = jnp.max(dlog, axis=-1)
    qk = jnp.einsum('bnhid,bnhjd->bnhij', q, k)
    wend = bcum[..., -1:] - bcum + li

    def step(carry, inp):
        C, nv, m = carry
        q_i, k_i, v_i, b_i, dl_i, dm_i, qk_i, we_i = inp
        inter = b_i + m[..., None]
        m_t = jnp.maximum(inter, dm_i)
        P = jnp.exp(dl_i - m_t[..., None]) * qk_i
        sc = jnp.exp(inter - m_t)
        numer = sc[..., None] * jnp.einsum('bhld,bhde->bhle', q_i, C) + jnp.einsum('bhij,bhje->bhie', P, v_i)
        denom = sc * jnp.einsum('bhld,bhd->bhl', q_i, nv) + jnp.sum(P, axis=-1)
        h_t = numer / jnp.maximum(jnp.abs(denom), jnp.exp(-m_t))[..., None]
        m_new = jnp.maximum(b_i[..., -1] + m, jnp.max(we_i, axis=-1))
        se = jnp.exp(b_i[..., -1] + m - m_new)
        kw = k_i * jnp.exp(we_i - m_new[..., None])[..., None]
        C = se[..., None, None] * C + jnp.einsum('bhld,bhle->bhde', kw, v_i)
        nv = se[..., None] * nv + jnp.sum(kw, axis=2)
        return (C, nv, m_new), h_t

    xs = tuple(jnp.moveaxis(t, 1, 0) for t in (q, k, v, bcum, dlog, dmax, qk, wend))
    init = (jnp.zeros((b, h, dqk, dv), jnp.float32), jnp.zeros((b, h, dqk), jnp.float32), jnp.zeros((b, h), jnp.float32))
    _, o = lax.scan(step, init, xs)
    return _from_chunks(o)


def soft_cap(z):
    return GATE_CAP * jnp.tanh(z / GATE_CAP)


def odd_mixer(h, w_in, w_out, b_i, b_f, onorm_g):
    bsz, s, _ = h.shape
    z = h @ w_in
    q, k, v, ig, fg, og = _split(z, ODD_SPLITS)
    q = q.reshape(bsz, s, C_HEADS, C_DQK)
    k = k.reshape(bsz, s, C_HEADS, C_DQK)
    v = v.reshape(bsz, s, C_HEADS, C_DV)
    li = soft_cap(ig.astype(jnp.float32) + b_i.astype(jnp.float32))
    lf = jax.nn.log_sigmoid(soft_cap(fg.astype(jnp.float32) + b_f.astype(jnp.float32)))
    ht = mlstm(q, k, v, li, lf)
    hh = jax.nn.sigmoid(og.astype(jnp.float32)).reshape(bsz, s, C_HEADS, C_DV) * ht
    hh = rms_norm(hh, onorm_g.reshape(C_HEADS, C_DV))
    return hh.reshape(bsz, s, ODD_OUT).astype(h.dtype) @ w_out


def setup_inputs(seed: int = 0) -> dict:
    key = jax.random.key(seed)
    ks = jax.random.split(key, 24)
    f32 = jnp.float32

    def nrm(k, shape, scale):
        return scale * jax.random.normal(k, shape, f32)

    x = nrm(ks[0], (BATCH, SEQ, D_MODEL), 1.0)
    norm_g = 1.0 + nrm(ks[1], (DEPTH, 6, D_MODEL), 0.02)
    ffn1_gu = nrm(ks[2], (DEPTH, D_MODEL, 2 * D_FF), D_MODEL ** -0.5)
    ffn1_down = nrm(ks[3], (DEPTH, D_FF, D_MODEL), D_FF ** -0.5)
    ffn2_gu = nrm(ks[4], (DEPTH, D_MODEL, 2 * D_FF), D_MODEL ** -0.5)
    ffn2_down = nrm(ks[5], (DEPTH, D_FF, D_MODEL), D_FF ** -0.5)
    ev_w_in = nrm(ks[6], (N_EVEN, D_MODEL, EVEN_IN), D_MODEL ** -0.5)
    ev_w_out = nrm(ks[7], (N_EVEN, EVEN_OUT, D_MODEL), EVEN_OUT ** -0.5)
    ev_a_ln_g = 1.0 + nrm(ks[8], (N_EVEN, A_WIDTH), 0.02)
    ev_a_ws = nrm(ks[9], (N_EVEN, A_GROUPS, A_CHUNK, A_CHUNK), A_CHUNK ** -0.5)
    ev_a_bs = 1.0 + nrm(ks[10], (N_EVEN, A_GROUPS, A_CHUNK), 0.02)
    ev_b_conv = nrm(ks[11], (N_EVEN, B_CONV, B_CONV_CH), B_CONV ** -0.5)
    ev_b_a_log = jnp.log(jax.random.uniform(ks[12], (N_EVEN, B_HEADS), f32, 1.0, 16.0))
    dt = jnp.exp(jax.random.uniform(ks[13], (N_EVEN, B_HEADS), f32, math.log(1e-3), math.log(1e-1)))
    ev_b_dt_bias = dt + jnp.log(-jnp.expm1(-dt))
    ev_b_onorm_g = 1.0 + nrm(ks[14], (N_EVEN, B_DV), 0.02)
    od_w_in = nrm(ks[15], (N_ODD, D_MODEL, ODD_IN), D_MODEL ** -0.5)
    od_w_out = nrm(ks[16], (N_ODD, ODD_OUT, D_MODEL), ODD_OUT ** -0.5)
    od_b_i = nrm(ks[17], (N_ODD, C_HEADS), 0.1)
    od_b_f = jnp.linspace(3.0, 6.0, C_HEADS, dtype=f32)[None, :] + nrm(ks[18], (N_ODD, C_HEADS), 0.1)
    od_onorm_g = 1.0 + nrm(ks[19], (N_ODD, C_HEADS * C_DV), 0.02)
    return {"x": x, "norm_g": norm_g, "ffn1_gu": ffn1_gu, "ffn1_down": ffn1_down,
            "ffn2_gu": ffn2_gu, "ffn2_down": ffn2_down,
            "ev_w_in": ev_w_in, "ev_w_out": ev_w_out, "ev_a_ln_g": ev_a_ln_g, "ev_a_ws": ev_a_ws,
            "ev_a_bs": ev_a_bs, "ev_b_conv": ev_b_conv, "ev_b_a_log": ev_b_a_log,
            "ev_b_dt_bias": ev_b_dt_bias, "ev_b_onorm_g": ev_b_onorm_g,
            "od_w_in": od_w_in, "od_w_out": od_w_out, "od_b_i": od_b_i, "od_b_f": od_b_f,
            "od_onorm_g": od_onorm_g}


def reference(x, norm_g, ffn1_gu, ffn1_down, ffn2_gu, ffn2_down,
              ev_w_in, ev_w_out, ev_a_ln_g, ev_a_ws, ev_a_bs, ev_b_conv, ev_b_a_log,
              ev_b_dt_bias, ev_b_onorm_g,
              od_w_in, od_w_out, od_b_i, od_b_f, od_onorm_g):
    for layer in range(DEPTH):
        ng = norm_g[layer]
        h = swiglu(rms_norm(x, ng[0]), ffn1_gu[layer], ffn1_down[layer])
        x = x + 0.5 * rms_norm(h, ng[1])
        h = rms_norm(x, ng[2])
        if layer % 2 == 0:
            j = layer // 2
            y = even_mixer(h, ev_w_in[j], ev_w_out[j], ev_a_ln_g[j], ev_a_ws[j], ev_a_bs[j],
                           ev_b_conv[j], ev_b_a_log[j], ev_b_dt_bias[j], ev_b_onorm_g[j])
        else:
            j = layer // 2
            y = odd_mixer(h, od_w_in[j], od_w_out[j], od_b_i[j], od_b_f[j], od_onorm_g[j])
        x = x + rms_norm(y, ng[3])
        h = swiglu(rms_norm(x, ng[4]), ffn2_gu[layer], ffn2_down[layer])
        x = x + 0.5 * rms_norm(h, ng[5])
    return x
```

```python
import functools

import jax
import jax.numpy as jnp
from jax import lax
from jax.experimental import pallas as pl
from jax.experimental.pallas import tpu as pltpu

F32 = jnp.float32
BF = jnp.bfloat16
HI = lax.Precision.HIGHEST

EPS = 1e-6
LANE = 128
CHUNK = 64
A_GROUPS, A_CHUNK = 4, 128
B_HEADS, B_CONV = 4, 4
C_HEADS, C_DQK = 8, 64
GATE_CAP = 15.0
VMEM_LIMIT = 56 * 1024 * 1024


def _dot(a, b, prec=None):
    return jnp.dot(a, b, preferred_element_type=F32, precision=prec)


def _dot_nt(a, b, prec=None):
    return lax.dot_general(a, b, (((1,), (1,)), ((), ())), preferred_element_type=F32, precision=prec)


def _rms(xf, g):
    return xf * lax.rsqrt(jnp.mean(xf * xf, axis=-1, keepdims=True) + EPS) * g


def _sigmoid(x):
    return 1.0 / (1.0 + jnp.exp(-x))


def _silu(x):
    return x * _sigmoid(x)


def _gelu(x):
    return 0.5 * x * (1.0 + jnp.tanh(0.7978845608028654 * (x + 0.044715 * (x * x * x))))


def _softplus(x):
    return jnp.maximum(x, 0.0) + jnp.log1p(jnp.exp(-jnp.abs(x)))


def _chunk_tri(n):
    r = lax.broadcasted_iota(jnp.int32, (n, n), 0)
    c = lax.broadcasted_iota(jnp.int32, (n, n), 1)
    same = lax.shift_right_logical(r, 6) == lax.shift_right_logical(c, 6)
    return jnp.where(same, jnp.where(c <= r, 1.0, 0.0), 0.0).astype(F32)


def _params(*sem):
    return pltpu.CompilerParams(dimension_semantics=sem, vmem_limit_bytes=VMEM_LIMIT)


def _vmem():
    return pl.BlockSpec(memory_space=pltpu.VMEM)


def _ffn_body(x_ref, gpre_ref, wg_ref, wu_ref, wd_ref, gpost_ref, o_ref, acc_ref, *, n_chunks):
    x = x_ref[...]
    xn = _rms(x, gpre_ref[...]).astype(BF)
    acc_ref[...] = jnp.zeros_like(acc_ref)

    def body(j, carry):
        g = _dot(xn, wg_ref[j])
        u = _dot(xn, wu_ref[j])
        h = (_silu(g) * u).astype(BF)
        acc_ref[...] += _dot(h, wd_ref[j])
        return carry

    lax.fori_loop(0, n_chunks, body, 0)
    o_ref[...] = x + 0.5 * _rms(acc_ref[...], gpost_ref[...])


def _ffn(x2, g_pre, w_gu, w_down, g_post, *, tm=512, tf=256):
    m, d = x2.shape
    f = w_down.shape[0]
    n = f // tf
    wgu = w_gu.astype(BF).reshape(d, 2, n, tf).transpose(1, 2, 0, 3)
    wd = w_down.astype(BF).reshape(n, tf, d)
    row = pl.BlockSpec((tm, d), lambda i: (i, 0))
    vec = pl.BlockSpec((1, d), lambda i: (0, 0))
    return pl.pallas_call(
        functools.partial(_ffn_body, n_chunks=n),
        grid=(m // tm,),
        in_specs=[row, vec, _vmem(), _vmem(), _vmem(), vec],
        out_specs=row,
        out_shape=jax.ShapeDtypeStruct((m, d), F32),
        scratch_shapes=[pltpu.VMEM((tm, d), F32)],
        compiler_params=_params("parallel"),
        name="ffn",
    )(x2, g_pre.reshape(1, d), wgu[0], wgu[1], wd, g_post.reshape(1, d))


def _post_body(y1_ref, y2_ref, w1_ref, w2_ref, g_ref, x_ref, o_ref):
    y = _dot(y1_ref[...], w1_ref[...]) + _dot(y2_ref[...], w2_ref[...])
    o_ref[...] = x_ref[...] + _rms(y, g_ref[...])


def _post(y1, c1, y2, c2, w_out, g_post, x2, *, tm=1024):
    m, d = x2.shape
    half = w_out.shape[0] // 2
    w = w_out.astype(BF)
    row = pl.BlockSpec((tm, d), lambda i: (i, 0))
    return pl.pallas_call(
        _post_body,
        grid=(m // tm,),
        in_specs=[pl.BlockSpec((tm, half), lambda i: (i, c1)), pl.BlockSpec((tm, half), lambda i: (i, c2)),
                  _vmem(), _vmem(), pl.BlockSpec((1, d), lambda i: (0, 0)), row],
        out_specs=row,
        out_shape=jax.ShapeDtypeStruct((m, d), F32),
        compiler_params=_params("parallel"),
        name="mixer_out",
    )(y1, y2, w[:half], w[half:], g_post.reshape(1, d), x2)


def _even_pre_body(x_ref, gpre_ref, wm_ref, ws_ref, wst_ref, lng_ref, aws_ref, bst_ref, cw_ref,
                   alog_l_ref, dtb_l_ref, alog_c_ref, dtb_c_ref,
                   y_ref, q_ref, k_ref, v_ref, gate_ref, colp_ref, rowp_ref, zbuf_ref, *, ts):
    aw = A_GROUPS * LANE
    xn = _rms(x_ref[...], gpre_ref[...]).astype(BF)

    u = _gelu(_dot(xn, wm_ref[:, 0:aw]))
    v = _gelu(_dot(xn, wm_ref[:, aw:2 * aw]))
    r = lax.broadcasted_iota(jnp.int32, (A_CHUNK, A_CHUNK), 0)
    c = lax.broadcasted_iota(jnp.int32, (A_CHUNK, A_CHUNK), 1)
    for g in range(A_GROUPS):
        gs = slice(g * LANE, (g + 1) * LANE)
        vg = v[:, gs]
        d = vg - jnp.mean(vg, axis=-1, keepdims=True)
        var = jnp.mean(d * d, axis=-1, keepdims=True)
        vn = (d * lax.rsqrt(var + EPS) * lng_ref[:, gs]).astype(BF)
        w = jnp.where(r >= c, aws_ref[g], 0.0).astype(BF)
        bcol = bst_ref[:, g:g + 1]
        for ci in range(ts // A_CHUNK):
            rs = slice(ci * A_CHUNK, (ci + 1) * A_CHUNK)
            mixed = _dot(w, vn[rs]) + bcol
            y_ref[rs, gs] = (u[rs, gs] * mixed).astype(BF)

    cq = 3 * B_HEADS * LANE

    @pl.when(pl.program_id(1) == 0)
    def _():
        zbuf_ref[0:8, :] = jnp.zeros((8, cq), F32)

    zbuf_ref[8:8 + ts, :] = _dot(xn, wm_ref[:, 2 * aw:2 * aw + cq])
    for t in range(3 * B_HEADS):
        cs = slice(t * LANE, (t + 1) * LANE)
        acc = cw_ref[0:1, cs] * zbuf_ref[8 - B_CONV + 1:8 - B_CONV + 1 + ts, cs]
        for j in range(1, B_CONV):
            off = 8 - B_CONV + 1 + j
            acc = acc + cw_ref[j:j + 1, cs] * zbuf_ref[off:off + ts, cs]
        a = _silu(acc)
        if t < 2 * B_HEADS:
            a = a * lax.rsqrt(jnp.sum(a * a, axis=-1, keepdims=True) + EPS)
        if t < B_HEADS:
            q_ref[:, cs] = (a * (LANE ** -0.5)).astype(BF)
        elif t < 2 * B_HEADS:
            k_ref[:, slice((t - B_HEADS) * LANE, (t - B_HEADS + 1) * LANE)] = a.astype(BF)
        else:
            v_ref[:, slice((t - 2 * B_HEADS) * LANE, (t - 2 * B_HEADS + 1) * LANE)] = a.astype(BF)
    zbuf_ref[0:8, :] = zbuf_ref[ts:ts + 8, :]
    gate_ref[...] = _dot(xn, wm_ref[:, 2 * aw + cq:]).astype(BF)

    zs = _dot(xn, ws_ref[...])
    tri = _chunk_tri(ts)
    gcol = -jnp.exp(alog_l_ref[...]) * _softplus(zs + dtb_l_ref[...])
    gc = _dot(tri, gcol, HI)
    lane = lax.broadcasted_iota(jnp.int32, (ts, LANE), 1)
    colp_ref[...] = jnp.where(lane < B_HEADS, _sigmoid(zs), gc)
    zr = _dot_nt(wst_ref[...], xn)
    grow = -jnp.exp(alog_c_ref[...]) * _softplus(zr + dtb_c_ref[...])
    gcr = _dot_nt(grow, tri, HI)
    for ci in range(ts // CHUNK):
        rowp_ref[ci] = gcr[:, ci * CHUNK:(ci + 1) * CHUNK]


def _even_pre(x3, g_pre, w_in, a_ln_g, a_ws, a_bs, b_conv, a_log, dt_bias, *, ts=512):
    b, s, d = x3.shape
    aw = A_GROUPS * LANE
    nq = 3 * B_HEADS * LANE
    o_beta = 2 * aw + nq
    wm = jnp.concatenate([w_in[:, :o_beta], w_in[:, o_beta + 2 * B_HEADS:]], axis=1).astype(BF)
    ws = jnp.pad(w_in[:, o_beta:o_beta + 2 * B_HEADS], ((0, 0), (0, LANE - 2 * B_HEADS))).astype(BF)
    wst = jnp.pad(w_in[:, o_beta + B_HEADS:o_beta + 2 * B_HEADS].T, ((0, 8 - B_HEADS), (0, 0))).astype(BF)
    lane_pad = lambda p: jnp.pad(p, (B_HEADS, LANE - 2 * B_HEADS)).reshape(1, LANE)
    sub_pad = lambda p: jnp.pad(p, (0, 8 - B_HEADS)).reshape(8, 1)
    grid = (b, s // ts)
    tok = lambda w: pl.BlockSpec((None, ts, w), lambda bi, i: (bi, i, 0))
    vec = lambda w: pl.BlockSpec((1, w), lambda bi, i: (0, 0))
    nck = ts // CHUNK
    outs = pl.pallas_call(
        functools.partial(_even_pre_body, ts=ts),
        grid=grid,
        in_specs=[tok(d), vec(d), _vmem(), _vmem(), _vmem(), vec(aw), _vmem(), _vmem(), _vmem(),
                  vec(LANE), vec(LANE), _vmem(), _vmem()],
        out_specs=[tok(aw), tok(aw), tok(aw), tok(aw), tok(aw), tok(LANE),
                   pl.BlockSpec((None, nck, 8, CHUNK), lambda bi, i: (bi, i, 0, 0))],
        out_shape=[jax.ShapeDtypeStruct((b, s, aw), BF)] * 5
        + [jax.ShapeDtypeStruct((b, s, LANE), F32), jax.ShapeDtypeStruct((b, s // CHUNK, 8, CHUNK), F32)],
        scratch_shapes=[pltpu.VMEM((ts + 8, nq), F32)],
        compiler_params=_params("parallel", "arbitrary"),
        name="even_pre",
    )(x3, g_pre.reshape(1, d), wm, ws, wst, a_ln_g.reshape(1, aw), a_ws, a_bs.T, b_conv,
      lane_pad(a_log), lane_pad(dt_bias), sub_pad(a_log), sub_pad(dt_bias))
    return outs


def _gdn_wy_body(q_ref, k_ref, v_ref, colp_ref, rowp_ref,
                 u_ref, w_ref, qg_ref, attn_ref, kdt_ref, dec_ref, *, nck):
    L = CHUNK
    r = lax.broadcasted_iota(jnp.int32, (L, L), 0)
    c = lax.broadcasted_iota(jnp.int32, (L, L), 1)
    causal = r >= c
    strict = r > c
    eye = jnp.where(r == c, 1.0, 0.0).astype(F32)

    def body(ci, carry):
        rows = pl.ds(pl.multiple_of(ci * L, L), L)
        colp = colp_ref[rows, :]
        rowp = rowp_ref[ci]
        for h in range(B_HEADS):
            hs = slice(h * LANE, (h + 1) * LANE)
            kh = k_ref[rows, hs].astype(F32)
            qh = q_ref[rows, hs].astype(F32)
            vh = v_ref[rows, hs].astype(F32)
            beta = colp[:, h:h + 1]
            gc = colp[:, B_HEADS + h:B_HEADS + h + 1]
            gr = rowp[h:h + 1, :]
            decay = jnp.exp(jnp.where(causal, gc - gr, -jnp.inf))
            kb = kh * beta
            khb = kh.astype(BF)
            kk = _dot_nt(kb.astype(BF), khb)
            pw = jnp.where(strict, -(kk * decay), 0.0)
            inv = eye + pw
            for _ in range(5):
                pw = _dot(pw, pw, HI)
                inv = inv + _dot(inv, pw, HI)
            eg = jnp.exp(gc)
            sol = _dot(inv, jnp.concatenate([vh * beta, kb * eg], axis=1), HI)
            u_ref[rows, hs] = sol[:, :LANE]
            w_ref[rows, hs] = sol[:, LANE:].astype(BF)
            qk = _dot_nt(qh.astype(BF), khb)
            attn_ref[h, rows, :] = jnp.where(causal, qk * decay, 0.0).astype(BF)
            qg_ref[rows, hs] = (qh * eg).astype(BF)
            gl = gc[L - 1:L, :]
            kd = kh * jnp.exp(gl - gc)
            kdt = jnp.concatenate([kd, jnp.zeros_like(kd)], axis=0).T
            kdt_ref[ci, h] = kdt[:, :L].astype(BF)
            dec_ref[ci, h:h + 1, :] = jnp.broadcast_to(jnp.exp(gl), (1, LANE))
        return carry

    lax.fori_loop(0, nck, body, 0)


def _gdn_wy(q, k, v, colp, rowp, *, ts=512):
    b, s, w = q.shape
    nck = ts // CHUNK
    grid = (b, s // ts)
    tok = lambda n: pl.BlockSpec((None, ts, n), lambda bi, i: (bi, i, 0))
    return pl.pallas_call(
        functools.partial(_gdn_wy_body, nck=nck),
        grid=grid,
        in_specs=[tok(w), tok(w), tok(w), tok(LANE),
                  pl.BlockSpec((None, nck, 8, CHUNK), lambda bi, i: (bi, i, 0, 0))],
        out_specs=[tok(w), tok(w), tok(w),
                   pl.BlockSpec((None, B_HEADS, ts, CHUNK), lambda bi, i: (bi, 0, i, 0)),
                   pl.BlockSpec((None, nck, B_HEADS, LANE, CHUNK), lambda bi, i: (bi, i, 0, 0, 0)),
                   pl.BlockSpec((None, nck, B_HEADS, LANE), lambda bi, i: (bi, i, 0, 0))],
        out_shape=[jax.ShapeDtypeStruct((b, s, w), F32), jax.ShapeDtypeStruct((b, s, w), BF),
                   jax.ShapeDtypeStruct((b, s, w), BF),
                   jax.ShapeDtypeStruct((b, B_HEADS, s, CHUNK), BF),
                   jax.ShapeDtypeStruct((b, s // CHUNK, B_HEADS, LANE, CHUNK), BF),
                   jax.ShapeDtypeStruct((b, s // CHUNK, B_HEADS, LANE), F32)],
        compiler_params=_params("parallel", "parallel"),
        name="gdn_wy",
    )(q, k, v, colp, rowp)


def _gdn_scan_body(u_ref, w_ref, qg_ref, attn_ref, kdt_ref, dec_ref, gate_ref, ong_ref,
                   y_ref, s_ref, *, nck):
    L = CHUNK

    @pl.when(pl.program_id(1) == 0)
    def _():
        s_ref[...] = jnp.zeros_like(s_ref)

    def body(ci, carry):
        rows = pl.ds(pl.multiple_of(ci * L, L), L)
        for h in range(B_HEADS):
            hs = slice(h * LANE, (h + 1) * LANE)
            st = s_ref[h]
            sb = st.astype(BF)
            vnew = u_ref[rows, hs] - _dot(w_ref[rows, hs], sb)
            vb = vnew.astype(BF)
            o = _dot(qg_ref[rows, hs], sb) + _dot(attn_ref[h, rows, :], vb)
            s_ref[h] = st * dec_ref[ci, h:h + 1, :] + _dot(kdt_ref[ci, h], vb)
            on = _rms(o, ong_ref[...])
            y_ref[rows, hs] = (on * _silu(gate_ref[rows, hs].astype(F32))).astype(BF)
        return carry

    lax.fori_loop(0, nck, body, 0)


def _gdn_scan(u, w, qg, attn, kdt, dec, gate, onorm_g, *, ts=512):
    b, s, wd = u.shape
    nck = ts // CHUNK
    tok = lambda n: pl.BlockSpec((None, ts, n), lambda bi, i: (bi, i, 0))
    return pl.pallas_call(
        functools.partial(_gdn_scan_body, nck=nck),
        grid=(b, s // ts),
        in_specs=[tok(wd), tok(wd), tok(wd),
                  pl.BlockSpec((None, B_HEADS, ts, CHUNK), lambda bi, i: (bi, 0, i, 0)),
                  pl.BlockSpec((None, nck, B_HEADS, LANE, CHUNK), lambda bi, i: (bi, i, 0, 0, 0)),
                  pl.BlockSpec((None, nck, B_HEADS, LANE), lambda bi, i: (bi, i, 0, 0)),
                  tok(wd), pl.BlockSpec((1, LANE), lambda bi, i: (0, 0))],
        out_specs=tok(wd),
        out_shape=jax.ShapeDtypeStruct((b, s, wd), BF),
        scratch_shapes=[pltpu.VMEM((B_HEADS, LANE, LANE), F32)],
        compiler_params=_params("parallel", "arbitrary"),
        name="gdn_scan",
    )(u, w, qg, attn, kdt, dec, gate, onorm_g.reshape(1, LANE))


def _odd_pre_body(x_ref, gpre_ref, wm_ref, ws_ref, wst_ref, bl_ref, bc_ref,
                  q_ref, k_ref, v_ref, og_ref, colp_ref, rowp_ref, *, ts):
    nqk = C_HEADS * C_DQK
    nv = C_HEADS * LANE
    xn = _rms(x_ref[...], gpre_ref[...]).astype(BF)
    q_ref[...] = _dot(xn, wm_ref[:, 0:nqk]).astype(BF)
    k_ref[...] = (_dot(xn, wm_ref[:, nqk:2 * nqk]) * (C_DQK ** -0.5)).astype(BF)
    v_ref[...] = _dot(xn, wm_ref[:, 2 * nqk:2 * nqk + nv]).astype(BF)
    og_ref[...] = _dot(xn, wm_ref[:, 2 * nqk + nv:]).astype(BF)

    tri = _chunk_tri(ts)
    cap = GATE_CAP * jnp.tanh((_dot(xn, ws_ref[...]) + bl_ref[...]) * (1.0 / GATE_CAP))
    bcum = _dot(tri, -_softplus(-cap), HI)
    lane = lax.broadcasted_iota(jnp.int32, (ts, LANE), 1)
    colp_ref[...] = jnp.where(lane < C_HEADS, cap, bcum)
    capr = GATE_CAP * jnp.tanh((_dot_nt(wst_ref[...], xn) + bc_ref[...]) * (1.0 / GATE_CAP))
    bcr = _dot_nt(-_softplus(-capr), tri, HI)
    sub = lax.broadcasted_iota(jnp.int32, (2 * C_HEADS, ts), 0)
    rp = jnp.where(sub < C_HEADS, capr, bcr)
    for ci in range(ts // CHUNK):
        rowp_ref[ci] = rp[:, ci * CHUNK:(ci + 1) * CHUNK]


def _odd_pre(x3, g_pre, w_in, b_i, b_f, *, ts=512):
    b, s, d = x3.shape
    nqk = C_HEADS * C_DQK
    nv = C_HEADS * LANE
    o_g = 2 * nqk + nv
    wm = jnp.concatenate([w_in[:, :o_g], w_in[:, o_g + 2 * C_HEADS:]], axis=1).astype(BF)
    wsm = w_in[:, o_g:o_g + 2 * C_HEADS]
    ws = jnp.pad(wsm, ((0, 0), (0, LANE - 2 * C_HEADS))).astype(BF)
    wst = wsm.T.astype(BF)
    bias = jnp.concatenate([b_i, b_f])
    tok = lambda n: pl.BlockSpec((None, ts, n), lambda bi, i: (bi, i, 0))
    vec = lambda n: pl.BlockSpec((1, n), lambda bi, i: (0, 0))
    nck = ts // CHUNK
    return pl.pallas_call(
        functools.partial(_odd_pre_body, ts=ts),
        grid=(b, s // ts),
        in_specs=[tok(d), vec(d), _vmem(), _vmem(), _vmem(), vec(LANE), _vmem()],
        out_specs=[tok(nqk), tok(nqk), tok(nv), tok(nv), tok(LANE),
                   pl.BlockSpec((None, nck, 2 * C_HEADS, CHUNK), lambda bi, i: (bi, i, 0, 0))],
        out_shape=[jax.ShapeDtypeStruct((b, s, nqk), BF), jax.ShapeDtypeStruct((b, s, nqk), BF),
                   jax.ShapeDtypeStruct((b, s, nv), BF), jax.ShapeDtypeStruct((b, s, nv), BF),
                   jax.ShapeDtypeStruct((b, s, LANE), F32),
                   jax.ShapeDtypeStruct((b, s // CHUNK, 2 * C_HEADS, CHUNK), F32)],
        compiler_params=_params("parallel", "parallel"),
        name="odd_pre",
    )(x3, g_pre.reshape(1, d), wm, ws, wst,
      jnp.pad(bias, (0, LANE - 2 * C_HEADS)).reshape(1, LANE), bias.reshape(2 * C_HEADS, 1))


def _mlstm_local_body(q_ref, k_ref, colp_ref, rowp_ref,
                      p_ref, qs_ref, kwt_ref, en_ref, se_ref, m_ref, *, nck):
    L = CHUNK
    r = lax.broadcasted_iota(jnp.int32, (L, L), 0)
    c = lax.broadcasted_iota(jnp.int32, (L, L), 1)
    causal = r >= c
    lane = lax.broadcasted_iota(jnp.int32, (L, LANE), 1)

    @pl.when(pl.program_id(1) == 0)
    def _():
        m_ref[...] = jnp.zeros_like(m_ref)

    def body(ci, carry):
        rows = pl.ds(pl.multiple_of(ci * L, L), L)
        colp = colp_ref[rows, :]
        rowp = rowp_ref[ci]
        en_tile = jnp.zeros((L, LANE), F32)
        for p in range(C_HEADS // 2):
            ps = slice(p * LANE, (p + 1) * LANE)
            qp = q_ref[rows, ps].astype(F32)
            kp = k_ref[rows, ps]
            kpf = kp.astype(F32)
            qs_pair = jnp.zeros((L, LANE), F32)
            kw_pair = jnp.zeros((L, LANE), F32)
            for sub in range(2):
                h = 2 * p + sub
                lm = (lane >= sub * C_DQK) & (lane < (sub + 1) * C_DQK)
                qm = jnp.where(lm, qp, 0.0)
                qk = _dot_nt(qm.astype(BF), kp)
                lic = colp[:, h:h + 1]
                bc = colp[:, C_HEADS + h:C_HEADS + h + 1]
                lir = rowp[h:h + 1, :]
                bcr = rowp[C_HEADS + h:C_HEADS + h + 1, :]
                dlog = jnp.where(causal, bc - bcr + lir, -jnp.inf)
                dmax = jnp.max(dlog, axis=-1, keepdims=True)
                m = m_ref[h:h + 1, 0:1]
                inter = bc + m
                mt = jnp.maximum(inter, dmax)
                p_ref[h, rows, :] = (jnp.exp(dlog - mt) * qk).astype(BF)
                qs_pair = jnp.where(lm, qp * jnp.exp(inter - mt), qs_pair)
                en_tile = jnp.where(lane == h, jnp.exp(-mt), en_tile)
                bl = bc[L - 1:L, :]
                wmax = jnp.max(bl - bcr + lir, axis=-1, keepdims=True)
                m_new = jnp.maximum(bl + m, wmax)
                kw_pair = jnp.where(lm, kpf * jnp.exp(bl - bc + lic - m_new), kw_pair)
                m_ref[h:h + 1, :] = jnp.broadcast_to(m_new, (1, LANE))
                se_ref[ci, h:h + 1, :] = jnp.broadcast_to(jnp.exp(bl + m - m_new), (1, LANE))
            qs_ref[rows, ps] = qs_pair.astype(BF)
            kwt = jnp.concatenate([kw_pair, jnp.zeros_like(kw_pair)], axis=0).T
            kwt_ref[ci, p] = kwt[:, :L].astype(BF)
        en_ref[rows, :] = en_tile
        return carry

    lax.fori_loop(0, nck, body, 0)


def _mlstm_local(q, k, colp, rowp, *, ts=512):
    b, s, nqk = q.shape
    nck = ts // CHUNK
    npair = C_HEADS // 2
    tok = lambda n: pl.BlockSpec((None, ts, n), lambda bi, i: (bi, i, 0))
    return pl.pallas_call(
        functools.partial(_mlstm_local_body, nck=nck),
        grid=(b, s // ts),
        in_specs=[tok(nqk), tok(nqk), tok(LANE),
                  pl.BlockSpec((None, nck, 2 * C_HEADS, CHUNK), lambda bi, i: (bi, i, 0, 0))],
        out_specs=[pl.BlockSpec((None, C_HEADS, ts, CHUNK), lambda bi, i: (bi, 0, i, 0)),
                   tok(nqk),
                   pl.BlockSpec((None, nck, npair, LANE, CHUNK), lambda bi, i: (bi, i, 0, 0, 0)),
                   tok(LANE),
                   pl.BlockSpec((None, nck, C_HEADS, LANE), lambda bi, i: (bi, i, 0, 0))],
        out_shape=[jax.ShapeDtypeStruct((b, C_HEADS, s, CHUNK), BF),
                   jax.ShapeDtypeStruct((b, s, nqk), BF),
                   jax.ShapeDtypeStruct((b, s // CHUNK, npair, LANE, CHUNK), BF),
                   jax.ShapeDtypeStruct((b, s, LANE), F32),
                   jax.ShapeDtypeStruct((b, s // CHUNK, C_HEADS, LANE), F32)],
        scratch_shapes=[pltpu.VMEM((C_HEADS, LANE), F32)],
        compiler_params=_params("parallel", "arbitrary"),
        name="mlstm_local",
    )(q, k, colp, rowp)


def _mlstm_scan_body(p_ref, qs_ref, kwt_ref, en_ref, se_ref, v_ref, og_ref, ong_ref,
                     y_ref, c_ref, *, nck):
    L = CHUNK
    lane = lax.broadcasted_iota(jnp.int32, (L, LANE), 1)
    ones = jnp.ones((L, LANE), BF)

    @pl.when(pl.program_id(1) == 0)
    def _():
        c_ref[...] = jnp.zeros_like(c_ref)

    def body(ci, carry):
        rows = pl.ds(pl.multiple_of(ci * L, L), L)
        for p in range(C_HEADS // 2):
            cb = c_ref[p].astype(BF)
            qsp = qs_ref[rows, p * LANE:(p + 1) * LANE]
            for sub in range(2):
                h = 2 * p + sub
                hs = slice(h * LANE, (h + 1) * LANE)
                sr = slice(sub * C_DQK, (sub + 1) * C_DQK)
                lm = (lane >= sub * C_DQK) & (lane < (sub + 1) * C_DQK)
                qsm = jnp.where(lm, qsp, jnp.zeros_like(qsp))
                vaug = jnp.concatenate([v_ref[rows, hs], ones], axis=1)
                res = _dot(qsm, cb) + _dot(p_ref[h, rows, :], vaug)
                den = jnp.maximum(jnp.abs(res[:, LANE:]), en_ref[rows, h:h + 1])
                hv = res[:, :LANE] / den
                se = se_ref[ci, h:h + 1, :]
                c_ref[p, sr, :] = (c_ref[p, sr, :] * jnp.concatenate([se, se], axis=1)
                                   + _dot(kwt_ref[ci, p, sr, :], vaug))
                hh = _sigmoid(og_ref[rows, hs].astype(F32)) * hv
                y_ref[rows, hs] = _rms(hh, ong_ref[:, hs]).astype(BF)
        return carry

    lax.fori_loop(0, nck, body, 0)


def _mlstm_scan(pm, qs, kwt, en, se, v, og, onorm_g, *, ts=512):
    b, s, nv = v.shape
    nqk = qs.shape[2]
    nck = ts // CHUNK
    npair = C_HEADS // 2
    tok = lambda n: pl.BlockSpec((None, ts, n), lambda bi, i: (bi, i, 0))
    return pl.pallas_call(
        functools.partial(_mlstm_scan_body, nck=nck),
        grid=(b, s // ts),
        in_specs=[pl.BlockSpec((None, C_HEADS, ts, CHUNK), lambda bi, i: (bi, 0, i, 0)),
                  tok(nqk),
                  pl.BlockSpec((None, nck, npair, LANE, CHUNK), lambda bi, i: (bi, i, 0, 0, 0)),
                  tok(LANE),
                  pl.BlockSpec((None, nck, C_HEADS, LANE), lambda bi, i: (bi, i, 0, 0)),
                  tok(nv), tok(nv), pl.BlockSpec((1, nv), lambda bi, i: (0, 0))],
        out_specs=tok(nv),
        out_shape=jax.ShapeDtypeStruct((b, s, nv), BF),
        scratch_shapes=[pltpu.VMEM((npair, LANE, 2 * LANE), F32)],
        compiler_params=_params("parallel", "arbitrary"),
        name="mlstm_scan",
    )(pm, qs, kwt, en, se, v, og, onorm_g.reshape(1, nv))


def kernel(x, norm_g, ffn1_gu, ffn1_down, ffn2_gu, ffn2_down, ev_w_in, ev_w_out, ev_a_ln_g, ev_a_ws, ev_a_bs, ev_b_conv, ev_b_a_log, ev_b_dt_bias, ev_b_onorm_g, od_w_in, od_w_out, od_b_i, od_b_f, od_onorm_g):
    b, s, d = x.shape
    x2 = x.reshape(b * s, d)
    depth = norm_g.shape[0]
    for layer in range(depth):
        ng = norm_g[layer]
        j = layer // 2
        x2 = _ffn(x2, ng[0], ffn1_gu[layer], ffn1_down[layer], ng[1])
        x3 = x2.reshape(b, s, d)
        if layer % 2 == 0:
            ya, q, k, v, gate, colp, rowp = _even_pre(
                x3, ng[2], ev_w_in[j], ev_a_ln_g[j], ev_a_ws[j], ev_a_bs[j], ev_b_conv[j],
                ev_b_a_log[j], ev_b_dt_bias[j])
            u, w, qg, attn, kdt, dec = _gdn_wy(q, k, v, colp, rowp)
            yb = _gdn_scan(u, w, qg, attn, kdt, dec, gate, ev_b_onorm_g[j])
            x2 = _post(ya.reshape(b * s, -1), 0, yb.reshape(b * s, -1), 0, ev_w_out[j], ng[3], x2)
        else:
            q, k, v, og, colp, rowp = _odd_pre(x3, ng[2], od_w_in[j], od_b_i[j], od_b_f[j])
            pm, qs, kwt, en, se = _mlstm_local(q, k, colp, rowp)
            y = _mlstm_scan(pm, qs, kwt, en, se, v, og, od_onorm_g[j]).reshape(b * s, -1)
            x2 = _post(y, 0, y, 1, od_w_out[j], ng[3], x2)
        x2 = _ffn(x2, ng[4], ffn2_gu[layer], ffn2_down[layer], ng[5])
    return x2.reshape(b, s, d)
```

```python
import functools

import jax
import jax.numpy as jnp
from jax import lax
from jax.experimental import pallas as pl
from jax.experimental.pallas import tpu as pltpu

F32 = jnp.float32
BF = jnp.bfloat16
HI = lax.Precision.HIGHEST

EPS = 1e-6
LANE = 128
CHUNK = 64
A_GROUPS, A_CHUNK = 4, 128
B_HEADS, B_CONV = 4, 4
C_HEADS, C_DQK = 8, 64
GATE_CAP = 15.0
VMEM_LIMIT = 56 * 1024 * 1024


def _dot(a, b, prec=None):
    return jnp.dot(a, b, preferred_element_type=F32, precision=prec)


def _dot_nt(a, b, prec=None):
    return lax.dot_general(a, b, (((1,), (1,)), ((), ())), preferred_element_type=F32, precision=prec)


def _rms(xf, g):
    return xf * lax.rsqrt(jnp.mean(xf * xf, axis=-1, keepdims=True) + EPS) * g


def _sigmoid(x):
    return 1.0 / (1.0 + jnp.exp(-x))


def _silu(x):
    return x * _sigmoid(x)


def _gelu(x):
    return 0.5 * x * (1.0 + jnp.tanh(0.7978845608028654 * (x + 0.044715 * (x * x * x))))


def _softplus(x):
    return jnp.maximum(x, 0.0) + jnp.log1p(jnp.exp(-jnp.abs(x)))


def _chunk_tri(n):
    r = lax.broadcasted_iota(jnp.int32, (n, n), 0)
    c = lax.broadcasted_iota(jnp.int32, (n, n), 1)
    same = lax.shift_right_logical(r, 6) == lax.shift_right_logical(c, 6)
    return jnp.where(same, jnp.where(c <= r, 1.0, 0.0), 0.0).astype(F32)


def _params(*sem):
    return pltpu.CompilerParams(dimension_semantics=sem, vmem_limit_bytes=VMEM_LIMIT)


def _vmem():
    return pl.BlockSpec(memory_space=pltpu.VMEM)


def _ffn_body(x_ref, gpre_ref, wg_ref, wu_ref, wd_ref, gpost_ref, o_ref, acc_ref, *, n_chunks):
    x = x_ref[...]
    xn = _rms(x, gpre_ref[...]).astype(BF)
    acc_ref[...] = jnp.zeros_like(acc_ref)

    def body(j, carry):
        g = _dot(xn, wg_ref[j])
        u = _dot(xn, wu_ref[j])
        h = (_silu(g) * u).astype(BF)
        acc_ref[...] += _dot(h, wd_ref[j])
        return carry

    lax.fori_loop(0, n_chunks, body, 0)
    o_ref[...] = x + 0.5 * _rms(acc_ref[...], gpost_ref[...])


def _ffn(x2, g_pre, w_gu, w_down, g_post, *, tm=512, tf=256):
    m, d = x2.shape
    f = w_down.shape[0]
    n = f // tf
    wgu = w_gu.astype(BF).reshape(d, 2, n, tf).transpose(1, 2, 0, 3)
    wd = w_down.astype(BF).reshape(n, tf, d)
    row = pl.BlockSpec((tm, d), lambda i: (i, 0))
    vec = pl.BlockSpec((1, d), lambda i: (0, 0))
    return pl.pallas_call(
        functools.partial(_ffn_body, n_chunks=n),
        grid=(m // tm,),
        in_specs=[row, vec, _vmem(), _vmem(), _vmem(), vec],
        out_specs=row,
        out_shape=jax.ShapeDtypeStruct((m, d), F32),
        scratch_shapes=[pltpu.VMEM((tm, d), F32)],
        compiler_params=_params("parallel"),
        name="ffn",
    )(x2, g_pre.reshape(1, d), wgu[0], wgu[1], wd, g_post.reshape(1, d))


def _post_body(y1_ref, y2_ref, w1_ref, w2_ref, g_ref, x_ref, o_ref):
    y = _dot(y1_ref[...], w1_ref[...]) + _dot(y2_ref[...], w2_ref[...])
    o_ref[...] = x_ref[...] + _rms(y, g_ref[...])


def _post(y1, c1, y2, c2, w_out, g_post, x2, *, tm=1024):
    m, d = x2.shape
    half = w_out.shape[0] // 2
    w = w_out.astype(BF)
    row = pl.BlockSpec((tm, d), lambda i: (i, 0))
    return pl.pallas_call(
        _post_body,
        grid=(m // tm,),
        in_specs=[pl.BlockSpec((tm, half), lambda i: (i, c1)), pl.BlockSpec((tm, half), lambda i: (i, c2)),
                  _vmem(), _vmem(), pl.BlockSpec((1, d), lambda i: (0, 0)), row],
        out_specs=row,
        out_shape=jax.ShapeDtypeStruct((m, d), F32),
        compiler_params=_params("parallel"),
        name="mixer_out",
    )(y1, y2, w[:half], w[half:], g_post.reshape(1, d), x2)


def _even_pre_body(x_ref, gpre_ref, wm_ref, ws_ref, wst_ref, lng_ref, aws_ref, bst_ref, cw_ref,
                   alog_l_ref, dtb_l_ref, alog_c_ref, dtb_c_ref,
                   y_ref, q_ref, k_ref, v_ref, gate_ref, colp_ref, rowp_ref, zbuf_ref, *, ts):
    aw = A_GROUPS * LANE
    xn = _rms(x_ref[...], gpre_ref[...]).astype(BF)

    u = _gelu(_dot(xn, wm_ref[:, 0:aw]))
    v = _gelu(_dot(xn, wm_ref[:, aw:2 * aw]))
    r = lax.broadcasted_iota(jnp.int32, (A_CHUNK, A_CHUNK), 0)
    c = lax.broadcasted_iota(jnp.int32, (A_CHUNK, A_CHUNK), 1)
    for g in range(A_GROUPS):
        gs = slice(g * LANE, (g + 1) * LANE)
        vg = v[:, gs]
        d = vg - jnp.mean(vg, axis=-1, keepdims=True)
        var = jnp.mean(d * d, axis=-1, keepdims=True)
        vn = (d * lax.rsqrt(var + EPS) * lng_ref[:, gs]).astype(BF)
        w = jnp.where(r >= c, aws_ref[g], 0.0).astype(BF)
        bcol = bst_ref[:, g:g + 1]
        for ci in range(ts // A_CHUNK):
            rs = slice(ci * A_CHUNK, (ci + 1) * A_CHUNK)
            mixed = _dot(w, vn[rs]) + bcol
            y_ref[rs, gs] = (u[rs, gs] * mixed).astype(BF)

    cq = 3 * B_HEADS * LANE

    @pl.when(pl.program_id(1) == 0)
    def _():
        zbuf_ref[0:8, :] = jnp.zeros((8, cq), F32)

    zbuf_ref[8:8 + ts, :] = _dot(xn, wm_ref[:, 2 * aw:2 * aw + cq])
    for t in range(3 * B_HEADS):
        cs = slice(t * LANE, (t + 1) * LANE)
        acc = cw_ref[0:1, cs] * zbuf_ref[8 - B_CONV + 1:8 - B_CONV + 1 + ts, cs]
        for j in range(1, B_CONV):
            off = 8 - B_CONV + 1 + j
            acc = acc + cw_ref[j:j + 1, cs] * zbuf_ref[off:off + ts, cs]
        a = _silu(acc)
        if t < 2 * B_HEADS:
            a = a * lax.rsqrt(jnp.sum(a * a, axis=-1, keepdims=True) + EPS)
        if t < B_HEADS:
            q_ref[:, cs] = (a * (LANE ** -0.5)).astype(BF)
        elif t < 2 * B_HEADS:
            k_ref[:, slice((t - B_HEADS) * LANE, (t - B_HEADS + 1) * LANE)] = a.astype(BF)
        else:
            v_ref[:, slice((t - 2 * B_HEADS) * LANE, (t - 2 * B_HEADS + 1) * LANE)] = a.astype(BF)
    zbuf_ref[0:8, :] = zbuf_ref[ts:ts + 8, :]
    gate_ref[...] = _dot(xn, wm_ref[:, 2 * aw + cq:]).astype(BF)

    zs = _dot(xn, ws_ref[...])
    tri = _chunk_tri(ts)
    gcol = -jnp.exp(alog_l_ref[...]) * _softplus(zs + dtb_l_ref[...])
    gc = _dot(tri, gcol, HI)
    lane = lax.broadcasted_iota(jnp.int32, (ts, LANE), 1)
    colp_ref[...] = jnp.where(lane < B_HEADS, _sigmoid(zs), gc)
    zr = _dot_nt(wst_ref[...], xn)
    grow = -jnp.exp(alog_c_ref[...]) * _softplus(zr + dtb_c_ref[...])
    gcr = _dot_nt(grow, tri, HI)
    for ci in range(ts // CHUNK):
        rowp_ref[ci] = gcr[:, ci * CHUNK:(ci + 1) * CHUNK]


def _even_pre(x3, g_pre, w_in, a_ln_g, a_ws, a_bs, b_conv, a_log, dt_bias, *, ts=512):
    b, s, d = x3.shape
    aw = A_GROUPS * LANE
    nq = 3 * B_HEADS * LANE
    o_beta = 2 * aw + nq
    wm = jnp.concatenate([w_in[:, :o_beta], w_in[:, o_beta + 2 * B_HEADS:]], axis=1).astype(BF)
    ws = jnp.pad(w_in[:, o_beta:o_beta + 2 * B_HEADS], ((0, 0), (0, LANE - 2 * B_HEADS))).astype(BF)
    wst = jnp.pad(w_in[:, o_beta + B_HEADS:o_beta + 2 * B_HEADS].T, ((0, 8 - B_HEADS), (0, 0))).astype(BF)
    lane_pad = lambda p: jnp.pad(p, (B_HEADS, LANE - 2 * B_HEADS)).reshape(1, LANE)
    sub_pad = lambda p: jnp.pad(p, (0, 8 - B_HEADS)).reshape(8, 1)
    grid = (b, s // ts)
    tok = lambda w: pl.BlockSpec((None, ts, w), lambda bi, i: (bi, i, 0))
    vec = lambda w: pl.BlockSpec((1, w), lambda bi, i: (0, 0))
    nck = ts // CHUNK
    outs = pl.pallas_call(
        functools.partial(_even_pre_body, ts=ts),
        grid=grid,
        in_specs=[tok(d), vec(d), _vmem(), _vmem(), _vmem(), vec(aw), _vmem(), _vmem(), _vmem(),
                  vec(LANE), vec(LANE), _vmem(), _vmem()],
        out_specs=[tok(aw), tok(aw), tok(aw), tok(aw), tok(aw), tok(LANE),
                   pl.BlockSpec((None, nck, 8, CHUNK), lambda bi, i: (bi, i, 0, 0))],
        out_shape=[jax.ShapeDtypeStruct((b, s, aw), BF)] * 5
        + [jax.ShapeDtypeStruct((b, s, LANE), F32), jax.ShapeDtypeStruct((b, s // CHUNK, 8, CHUNK), F32)],
        scratch_shapes=[pltpu.VMEM((ts + 8, nq), F32)],
        compiler_params=_params("parallel", "arbitrary"),
        name="even_pre",
    )(x3, g_pre.reshape(1, d), wm, ws, wst, a_ln_g.reshape(1, aw), a_ws, a_bs.T, b_conv,
      lane_pad(a_log), lane_pad(dt_bias), sub_pad(a_log), sub_pad(dt_bias))
    return outs


def _gdn_wy_body(q_ref, k_ref, v_ref, colp_ref, rowp_ref,
                 u_ref, w_ref, qg_ref, attn_ref, kdt_ref, dec_ref, *, nck, cg):
    L = CHUNK
    r = lax.broadcasted_iota(jnp.int32, (L, L), 0)
    c = lax.broadcasted_iota(jnp.int32, (L, L), 1)
    causal = r >= c
    strict = r > c
    eye = jnp.where(r == c, 1.0, 0.0).astype(F32)
    shr = lax.shift_right_logical
    diag8 = jnp.where(shr(r, 3) == shr(c, 3), 1.0, 0.0).astype(F32)
    off = [jnp.where((shr(r, t + 1) == shr(c, t + 1)) & (shr(r, t) > shr(c, t)), 1.0, 0.0).astype(F32)
           for t in (3, 4, 5)]

    def mm(a, b):
        return _dot(a.astype(BF), b.astype(BF))

    def group(units):
        n = range(len(units))
        rows = [slice(ci * L, (ci + 1) * L) for ci, _ in units]
        hs = [slice(h * LANE, (h + 1) * LANE) for _, h in units]
        kh = [k_ref[rows[i], hs[i]] for i in n]
        khf = [kh[i].astype(F32) for i in n]
        beta = [colp_ref[rows[i], h:h + 1] for i, (_, h) in enumerate(units)]
        gc = [colp_ref[rows[i], B_HEADS + h:B_HEADS + h + 1] for i, (_, h) in enumerate(units)]
        decay = [jnp.exp(jnp.where(causal, gc[i] - rowp_ref[ci, h:h + 1, :], -jnp.inf))
                 for i, (ci, h) in enumerate(units)]
        kb = [khf[i] * beta[i] for i in n]
        kk = [_dot_nt(kb[i].astype(BF), kh[i]) for i in n]
        qk = [_dot_nt(q_ref[rows[i], hs[i]], kh[i]) for i in n]
        low = [jnp.where(strict, kk[i] * decay[i], 0.0) for i in n]
        for i, (_, h) in enumerate(units):
            attn_ref[h, rows[i], :] = jnp.where(causal, qk[i] * decay[i], 0.0).astype(BF)
        pw = [-(low[i] * diag8) for i in n]
        inv = [eye + pw[i] for i in n]
        for _ in range(2):
            pw = [mm(pw[i], pw[i]) for i in n]
            t = [mm(inv[i], pw[i]) for i in n]
            inv = [inv[i] + t[i] for i in n]
        for m in off:
            t = [mm(low[i] * m, inv[i]) for i in n]
            t = [mm(inv[i], t[i]) for i in n]
            inv = [inv[i] - t[i] for i in n]
        eg = [jnp.exp(gc[i]) for i in n]
        sol = [mm(inv[i], jnp.concatenate([v_ref[rows[i], hs[i]].astype(F32) * beta[i], kb[i] * eg[i]], axis=1))
               for i in n]
        for i, (ci, h) in enumerate(units):
            u_ref[rows[i], hs[i]] = sol[i][:, :LANE]
            w_ref[rows[i], hs[i]] = sol[i][:, LANE:].astype(BF)
            qg_ref[rows[i], hs[i]] = (q_ref[rows[i], hs[i]].astype(F32) * eg[i]).astype(BF)
            gl = gc[i][L - 1:L, :]
            kd = khf[i] * jnp.exp(gl - gc[i])
            kdt = jnp.concatenate([kd, jnp.zeros_like(kd)], axis=0).T
            kdt_ref[ci, h] = kdt[:, :L].astype(BF)
            dec_ref[ci, h:h + 1, :] = jnp.broadcast_to(jnp.exp(gl), (1, LANE))

    for c0 in range(0, nck, cg):
        group([(ci, h) for ci in range(c0, c0 + cg) for h in range(B_HEADS)])


def _gdn_wy(q, k, v, colp, rowp, *, ts=512, cg=4):
    b, s, w = q.shape
    nck = ts // CHUNK
    grid = (b, s // ts)
    tok = lambda n: pl.BlockSpec((None, ts, n), lambda bi, i: (bi, i, 0))
    return pl.pallas_call(
        functools.partial(_gdn_wy_body, nck=nck, cg=cg),
        grid=grid,
        in_specs=[tok(w), tok(w), tok(w), tok(LANE),
                  pl.BlockSpec((None, nck, 8, CHUNK), lambda bi, i: (bi, i, 0, 0))],
        out_specs=[tok(w), tok(w), tok(w),
                   pl.BlockSpec((None, B_HEADS, ts, CHUNK), lambda bi, i: (bi, 0, i, 0)),
                   pl.BlockSpec((None, nck, B_HEADS, LANE, CHUNK), lambda bi, i: (bi, i, 0, 0, 0)),
                   pl.BlockSpec((None, nck, B_HEADS, LANE), lambda bi, i: (bi, i, 0, 0))],
        out_shape=[jax.ShapeDtypeStruct((b, s, w), F32), jax.ShapeDtypeStruct((b, s, w), BF),
                   jax.ShapeDtypeStruct((b, s, w), BF),
                   jax.ShapeDtypeStruct((b, B_HEADS, s, CHUNK), BF),
                   jax.ShapeDtypeStruct((b, s // CHUNK, B_HEADS, LANE, CHUNK), BF),
                   jax.ShapeDtypeStruct((b, s // CHUNK, B_HEADS, LANE), F32)],
        compiler_params=_params("parallel", "parallel"),
        name="gdn_wy",
    )(q, k, v, colp, rowp)


def _gdn_scan_body(u_ref, w_ref, qg_ref, attn_ref, kdt_ref, dec_ref, gate_ref, ong_ref,
                   y_ref, s_ref, *, nck, bb):
    L = CHUNK

    @pl.when(pl.program_id(1) == 0)
    def _():
        s_ref[...] = jnp.zeros_like(s_ref)

    def body(ci, carry):
        rows = pl.ds(pl.multiple_of(ci * L, L), L)
        ch = [(bi, h, slice(h * LANE, (h + 1) * LANE)) for bi in range(bb) for h in range(B_HEADS)]
        sb = [s_ref[bi, h].astype(BF) for bi, h, _ in ch]
        ws = [_dot(w_ref[bi, rows, hs], sb[i]) for i, (bi, _, hs) in enumerate(ch)]
        qs = [_dot(qg_ref[bi, rows, hs], sb[i]) for i, (bi, _, hs) in enumerate(ch)]
        vb = [(u_ref[bi, rows, hs] - ws[i]).astype(BF) for i, (bi, _, hs) in enumerate(ch)]
        upd = [_dot(kdt_ref[bi, ci, h], vb[i]) for i, (bi, h, _) in enumerate(ch)]
        av = [_dot(attn_ref[bi, h, rows, :], vb[i]) for i, (bi, h, _) in enumerate(ch)]
        for i, (bi, h, hs) in enumerate(ch):
            s_ref[bi, h] = s_ref[bi, h] * dec_ref[bi, ci, h:h + 1, :] + upd[i]
        for i, (bi, h, hs) in enumerate(ch):
            on = _rms(qs[i] + av[i], ong_ref[...])
            y_ref[bi, rows, hs] = (on * _silu(gate_ref[bi, rows, hs].astype(F32))).astype(BF)
        return carry

    lax.fori_loop(0, nck, body, 0)


def _gdn_scan(u, w, qg, attn, kdt, dec, gate, onorm_g, *, ts=256, bb=8):
    b, s, wd = u.shape
    bb = min(bb, b)
    nck = ts // CHUNK
    tok = lambda n: pl.BlockSpec((bb, ts, n), lambda bi, i: (bi, i, 0))
    return pl.pallas_call(
        functools.partial(_gdn_scan_body, nck=nck, bb=bb),
        grid=(b // bb, s // ts),
        in_specs=[tok(wd), tok(wd), tok(wd),
                  pl.BlockSpec((bb, B_HEADS, ts, CHUNK), lambda bi, i: (bi, 0, i, 0)),
                  pl.BlockSpec((bb, nck, B_HEADS, LANE, CHUNK), lambda bi, i: (bi, i, 0, 0, 0)),
                  pl.BlockSpec((bb, nck, B_HEADS, LANE), lambda bi, i: (bi, i, 0, 0)),
                  tok(wd), pl.BlockSpec((1, LANE), lambda bi, i: (0, 0))],
        out_specs=tok(wd),
        out_shape=jax.ShapeDtypeStruct((b, s, wd), BF),
        scratch_shapes=[pltpu.VMEM((bb, B_HEADS, LANE, LANE), F32)],
        compiler_params=_params("parallel", "arbitrary"),
        name="gdn_scan",
    )(u, w, qg, attn, kdt, dec, gate, onorm_g.reshape(1, LANE))


def _odd_pre_body(x_ref, gpre_ref, wm_ref, ws_ref, wst_ref, bl_ref, bc_ref,
                  q_ref, k_ref, v_ref, og_ref, colp_ref, rowp_ref, *, ts):
    nqk = C_HEADS * C_DQK
    nv = C_HEADS * LANE
    xn = _rms(x_ref[...], gpre_ref[...]).astype(BF)
    q_ref[...] = _dot(xn, wm_ref[:, 0:nqk]).astype(BF)
    k_ref[...] = (_dot(xn, wm_ref[:, nqk:2 * nqk]) * (C_DQK ** -0.5)).astype(BF)
    v_ref[...] = _dot(xn, wm_ref[:, 2 * nqk:2 * nqk + nv]).astype(BF)
    og_ref[...] = _dot(xn, wm_ref[:, 2 * nqk + nv:]).astype(BF)

    tri = _chunk_tri(ts)
    cap = GATE_CAP * jnp.tanh((_dot(xn, ws_ref[...]) + bl_ref[...]) * (1.0 / GATE_CAP))
    bcum = _dot(tri, -_softplus(-cap), HI)
    lane = lax.broadcasted_iota(jnp.int32, (ts, LANE), 1)
    colp_ref[...] = jnp.where(lane < C_HEADS, cap, bcum)
    capr = GATE_CAP * jnp.tanh((_dot_nt(wst_ref[...], xn) + bc_ref[...]) * (1.0 / GATE_CAP))
    bcr = _dot_nt(-_softplus(-capr), tri, HI)
    sub = lax.broadcasted_iota(jnp.int32, (2 * C_HEADS, ts), 0)
    rp = jnp.where(sub < C_HEADS, capr, bcr)
    for ci in range(ts // CHUNK):
        rowp_ref[ci] = rp[:, ci * CHUNK:(ci + 1) * CHUNK]


def _odd_pre(x3, g_pre, w_in, b_i, b_f, *, ts=512):
    b, s, d = x3.shape
    nqk = C_HEADS * C_DQK
    nv = C_HEADS * LANE
    o_g = 2 * nqk + nv
    wm = jnp.concatenate([w_in[:, :o_g], w_in[:, o_g + 2 * C_HEADS:]], axis=1).astype(BF)
    wsm = w_in[:, o_g:o_g + 2 * C_HEADS]
    ws = jnp.pad(wsm, ((0, 0), (0, LANE - 2 * C_HEADS))).astype(BF)
    wst = wsm.T.astype(BF)
    bias = jnp.concatenate([b_i, b_f])
    tok = lambda n: pl.BlockSpec((None, ts, n), lambda bi, i: (bi, i, 0))
    vec = lambda n: pl.BlockSpec((1, n), lambda bi, i: (0, 0))
    nck = ts // CHUNK
    return pl.pallas_call(
        functools.partial(_odd_pre_body, ts=ts),
        grid=(b, s // ts),
        in_specs=[tok(d), vec(d), _vmem(), _vmem(), _vmem(), vec(LANE), _vmem()],
        out_specs=[tok(nqk), tok(nqk), tok(nv), tok(nv), tok(LANE),
                   pl.BlockSpec((None, nck, 2 * C_HEADS, CHUNK), lambda bi, i: (bi, i, 0, 0))],
        out_shape=[jax.ShapeDtypeStruct((b, s, nqk), BF), jax.ShapeDtypeStruct((b, s, nqk), BF),
                   jax.ShapeDtypeStruct((b, s, nv), BF), jax.ShapeDtypeStruct((b, s, nv), BF),
                   jax.ShapeDtypeStruct((b, s, LANE), F32),
                   jax.ShapeDtypeStruct((b, s // CHUNK, 2 * C_HEADS, CHUNK), F32)],
        compiler_params=_params("parallel", "parallel"),
        name="odd_pre",
    )(x3, g_pre.reshape(1, d), wm, ws, wst,
      jnp.pad(bias, (0, LANE - 2 * C_HEADS)).reshape(1, LANE), bias.reshape(2 * C_HEADS, 1))


def _mlstm_local_body(q_ref, k_ref, colp_ref, rowp_ref,
                      p_ref, qs_ref, kwt_ref, en_ref, se_ref, m_ref, *, nck):
    L = CHUNK
    r = lax.broadcasted_iota(jnp.int32, (L, L), 0)
    c = lax.broadcasted_iota(jnp.int32, (L, L), 1)
    causal = r >= c
    lane = lax.broadcasted_iota(jnp.int32, (L, LANE), 1)

    @pl.when(pl.program_id(1) == 0)
    def _():
        m_ref[...] = jnp.zeros_like(m_ref)

    heads = range(C_HEADS)
    hmask = [(lane >= (h % 2) * C_DQK) & (lane < (h % 2 + 1) * C_DQK) for h in heads]
    ms = [m_ref[h:h + 1, 0:1] for h in heads]
    for ci in range(nck):
        rows = slice(ci * L, (ci + 1) * L)
        colp = colp_ref[rows, :]
        rowp = rowp_ref[ci]
        ps = [slice((h // 2) * LANE, (h // 2 + 1) * LANE) for h in heads]
        qk = [_dot_nt(jnp.where(hmask[h], q_ref[rows, ps[h]], jnp.zeros((L, LANE), BF)), k_ref[rows, ps[h]])
              for h in heads]
        lic = [colp[:, h:h + 1] for h in heads]
        bc = [colp[:, C_HEADS + h:C_HEADS + h + 1] for h in heads]
        bl = [bc[h][L - 1:L, :] for h in heads]
        dlog = [jnp.where(causal, bc[h] - rowp[C_HEADS + h:C_HEADS + h + 1, :] + rowp[h:h + 1, :], -jnp.inf)
                for h in heads]
        dmax = [jnp.max(dlog[h], axis=-1, keepdims=True) for h in heads]
        wmax = [jnp.max(bl[h] - rowp[C_HEADS + h:C_HEADS + h + 1, :] + rowp[h:h + 1, :], axis=-1, keepdims=True)
                for h in heads]
        m_old = list(ms)
        ms = [jnp.maximum(bl[h] + m_old[h], wmax[h]) for h in heads]
        inter = [bc[h] + m_old[h] for h in heads]
        mt = [jnp.maximum(inter[h], dmax[h]) for h in heads]
        en_tile = jnp.zeros((L, LANE), F32)
        for h in heads:
            p_ref[h, rows, :] = (jnp.exp(dlog[h] - mt[h]) * qk[h]).astype(BF)
            en_tile = jnp.where(lane == h, jnp.exp(-mt[h]), en_tile)
            se_ref[ci, h:h + 1, :] = jnp.broadcast_to(jnp.exp(bl[h] + m_old[h] - ms[h]), (1, LANE))
        en_ref[rows, :] = en_tile
        for p in range(C_HEADS // 2):
            h0, h1 = 2 * p, 2 * p + 1
            qp = q_ref[rows, ps[h0]].astype(F32)
            kpf = k_ref[rows, ps[h0]].astype(F32)
            qs_ref[rows, ps[h0]] = (qp * jnp.where(hmask[h0], jnp.exp(inter[h0] - mt[h0]),
                                                   jnp.exp(inter[h1] - mt[h1]))).astype(BF)
            kw = kpf * jnp.where(hmask[h0], jnp.exp(bl[h0] - bc[h0] + lic[h0] - ms[h0]),
                                 jnp.exp(bl[h1] - bc[h1] + lic[h1] - ms[h1]))
            kwt = jnp.concatenate([kw, jnp.zeros_like(kw)], axis=0).T
            kwt_ref[ci, p] = kwt[:, :L].astype(BF)
    for h in heads:
        m_ref[h:h + 1, :] = jnp.broadcast_to(ms[h], (1, LANE))


def _mlstm_local(q, k, colp, rowp, *, ts=512):
    b, s, nqk = q.shape
    nck = ts // CHUNK
    npair = C_HEADS // 2
    tok = lambda n: pl.BlockSpec((None, ts, n), lambda bi, i: (bi, i, 0))
    return pl.pallas_call(
        functools.partial(_mlstm_local_body, nck=nck),
        grid=(b, s // ts),
        in_specs=[tok(nqk), tok(nqk), tok(LANE),
                  pl.BlockSpec((None, nck, 2 * C_HEADS, CHUNK), lambda bi, i: (bi, i, 0, 0))],
        out_specs=[pl.BlockSpec((None, C_HEADS, ts, CHUNK), lambda bi, i: (bi, 0, i, 0)),
                   tok(nqk),
                   pl.BlockSpec((None, nck, npair, LANE, CHUNK), lambda bi, i: (bi, i, 0, 0, 0)),
                   tok(LANE),
                   pl.BlockSpec((None, nck, C_HEADS, LANE), lambda bi, i: (bi, i, 0, 0))],
        out_shape=[jax.ShapeDtypeStruct((b, C_HEADS, s, CHUNK), BF),
                   jax.ShapeDtypeStruct((b, s, nqk), BF),
                   jax.ShapeDtypeStruct((b, s // CHUNK, npair, LANE, CHUNK), BF),
                   jax.ShapeDtypeStruct((b, s, LANE), F32),
                   jax.ShapeDtypeStruct((b, s // CHUNK, C_HEADS, LANE), F32)],
        scratch_shapes=[pltpu.VMEM((C_HEADS, LANE), F32)],
        compiler_params=_params("parallel", "arbitrary"),
        name="mlstm_local",
    )(q, k, colp, rowp)


def _mlstm_scan_body(p_ref, qs_ref, kwt_ref, en_ref, se_ref, v_ref, og_ref, ong_ref,
                     y_ref, c_ref, *, nck, bb):
    L = CHUNK
    lane = lax.broadcasted_iota(jnp.int32, (L, LANE), 1)
    ones = jnp.ones((L, LANE), BF)

    @pl.when(pl.program_id(1) == 0)
    def _():
        c_ref[...] = jnp.zeros_like(c_ref)

    def body(ci, carry):
        rows = pl.ds(pl.multiple_of(ci * L, L), L)
        ch = [(bi, h) for bi in range(bb) for h in range(C_HEADS)]
        hsl = lambda h: slice(h * LANE, (h + 1) * LANE)
        srl = lambda h: slice((h % 2) * C_DQK, (h % 2 + 1) * C_DQK)
        cb = {(bi, p): c_ref[bi, p].astype(BF) for bi in range(bb) for p in range(C_HEADS // 2)}
        vaug = [jnp.concatenate([v_ref[bi, rows, hsl(h)], ones], axis=1) for bi, h in ch]
        res, upd = [], []
        for i, (bi, h) in enumerate(ch):
            qsp = qs_ref[bi, rows, hsl(h // 2)]
            lm = (lane >= (h % 2) * C_DQK) & (lane < (h % 2 + 1) * C_DQK)
            qsm = jnp.where(lm, qsp, jnp.zeros_like(qsp))
            res.append(_dot(qsm, cb[(bi, h // 2)]) + _dot(p_ref[bi, h, rows, :], vaug[i]))
        for i, (bi, h) in enumerate(ch):
            upd.append(_dot(kwt_ref[bi, ci, h // 2, srl(h), :], vaug[i]))
        for i, (bi, h) in enumerate(ch):
            se = se_ref[bi, ci, h:h + 1, :]
            c_ref[bi, h // 2, srl(h), :] = (c_ref[bi, h // 2, srl(h), :] * jnp.concatenate([se, se], axis=1)
                                            + upd[i])
        for i, (bi, h) in enumerate(ch):
            den = jnp.maximum(jnp.abs(res[i][:, LANE:]), en_ref[bi, rows, h:h + 1])
            hh = _sigmoid(og_ref[bi, rows, hsl(h)].astype(F32)) * (res[i][:, :LANE] / den)
            y_ref[bi, rows, hsl(h)] = _rms(hh, ong_ref[:, hsl(h)]).astype(BF)
        return carry

    lax.fori_loop(0, nck, body, 0)


def _mlstm_scan(pm, qs, kwt, en, se, v, og, onorm_g, *, ts=256, bb=4):
    b, s, nv = v.shape
    bb = min(bb, b)
    nqk = qs.shape[2]
    nck = ts // CHUNK
    npair = C_HEADS // 2
    tok = lambda n: pl.BlockSpec((bb, ts, n), lambda bi, i: (bi, i, 0))
    return pl.pallas_call(
        functools.partial(_mlstm_scan_body, nck=nck, bb=bb),
        grid=(b // bb, s // ts),
        in_specs=[pl.BlockSpec((bb, C_HEADS, ts, CHUNK), lambda bi, i: (bi, 0, i, 0)),
                  tok(nqk),
                  pl.BlockSpec((bb, nck, npair, LANE, CHUNK), lambda bi, i: (bi, i, 0, 0, 0)),
                  tok(LANE),
                  pl.BlockSpec((bb, nck, C_HEADS, LANE), lambda bi, i: (bi, i, 0, 0)),
                  tok(nv), tok(nv), pl.BlockSpec((1, nv), lambda bi, i: (0, 0))],
        out_specs=tok(nv),
        out_shape=jax.ShapeDtypeStruct((b, s, nv), BF),
        scratch_shapes=[pltpu.VMEM((bb, npair, LANE, 2 * LANE), F32)],
        compiler_params=_params("parallel", "arbitrary"),
        name="mlstm_scan",
    )(pm, qs, kwt, en, se, v, og, onorm_g.reshape(1, nv))


def kernel(x, norm_g, ffn1_gu, ffn1_down, ffn2_gu, ffn2_down, ev_w_in, ev_w_out, ev_a_ln_g, ev_a_ws, ev_a_bs, ev_b_conv, ev_b_a_log, ev_b_dt_bias, ev_b_onorm_g, od_w_in, od_w_out, od_b_i, od_b_f, od_onorm_g):
    b, s, d = x.shape
    x2 = x.reshape(b * s, d)
    depth = norm_g.shape[0]
    for layer in range(depth):
        ng = norm_g[layer]
        j = layer // 2
        x2 = _ffn(x2, ng[0], ffn1_gu[layer], ffn1_down[layer], ng[1])
        x3 = x2.reshape(b, s, d)
        if layer % 2 == 0:
            ya, q, k, v, gate, colp, rowp = _even_pre(
                x3, ng[2], ev_w_in[j], ev_a_ln_g[j], ev_a_ws[j], ev_a_bs[j], ev_b_conv[j],
                ev_b_a_log[j], ev_b_dt_bias[j])
            u, w, qg, attn, kdt, dec = _gdn_wy(q, k, v, colp, rowp)
            yb = _gdn_scan(u, w, qg, attn, kdt, dec, gate, ev_b_onorm_g[j])
            x2 = _post(ya.reshape(b * s, -1), 0, yb.reshape(b * s, -1), 0, ev_w_out[j], ng[3], x2)
        else:
            q, k, v, og, colp, rowp = _odd_pre(x3, ng[2], od_w_in[j], od_b_i[j], od_b_f[j])
            pm, qs, kwt, en, se = _mlstm_local(q, k, colp, rowp)
            y = _mlstm_scan(pm, qs, kwt, en, se, v, og, od_onorm_g[j]).reshape(b * s, -1)
            x2 = _post(y, 0, y, 1, od_w_out[j], ng[3], x2)
        x2 = _ffn(x2, ng[4], ffn2_gu[layer], ffn2_down[layer], ng[5])
    return x2.reshape(b, s, d)
```

```python
import functools

import jax
import jax.numpy as jnp
from jax import lax
from jax.experimental import pallas as pl
from jax.experimental.pallas import tpu as pltpu

F32 = jnp.float32
BF = jnp.bfloat16
HI = lax.Precision.HIGHEST

EPS = 1e-6
LANE = 128
CHUNK = 64
A_GROUPS, A_CHUNK = 4, 128
B_HEADS, B_CONV = 4, 4
C_HEADS, C_DQK = 8, 64
GATE_CAP = 15.0
VMEM_LIMIT = 56 * 1024 * 1024


def _dot(a, b, prec=None):
    return jnp.dot(a, b, preferred_element_type=F32, precision=prec)


def _dot_nt(a, b, prec=None):
    return lax.dot_general(a, b, (((1,), (1,)), ((), ())), preferred_element_type=F32, precision=prec)


def _rms(xf, g):
    return xf * lax.rsqrt(jnp.mean(xf * xf, axis=-1, keepdims=True) + EPS) * g


def _sigmoid(x):
    return 1.0 / (1.0 + jnp.exp(-x))


def _silu(x):
    return x * _sigmoid(x)


def _gelu(x):
    return 0.5 * x * (1.0 + jnp.tanh(0.7978845608028654 * (x + 0.044715 * (x * x * x))))


def _softplus(x):
    return jnp.maximum(x, 0.0) + jnp.log1p(jnp.exp(-jnp.abs(x)))


def _chunk_tri(n):
    r = lax.broadcasted_iota(jnp.int32, (n, n), 0)
    c = lax.broadcasted_iota(jnp.int32, (n, n), 1)
    same = lax.shift_right_logical(r, 6) == lax.shift_right_logical(c, 6)
    return jnp.where(same, jnp.where(c <= r, 1.0, 0.0), 0.0).astype(F32)


def _params(*sem):
    return pltpu.CompilerParams(dimension_semantics=sem, vmem_limit_bytes=VMEM_LIMIT)


def _vmem():
    return pl.BlockSpec(memory_space=pltpu.VMEM)


def _ffn_body(x_ref, gpre_ref, wg_ref, wu_ref, wd_ref, gpost_ref, o_ref, acc_ref, *, n_chunks):
    x = x_ref[...]
    xn = _rms(x, gpre_ref[...]).astype(BF)
    gu = (_dot(xn, wg_ref[0]), _dot(xn, wu_ref[0]))
    for j in range(n_chunks):
        g, u = gu
        if j + 1 < n_chunks:
            gu = (_dot(xn, wg_ref[j + 1]), _dot(xn, wu_ref[j + 1]))
        d = _dot((_silu(g) * u).astype(BF), wd_ref[j])
        if j == 0:
            acc_ref[...] = d
        else:
            acc_ref[...] += d
    o_ref[...] = x + _rms(acc_ref[...], 0.5 * gpost_ref[...])


def _ffn(x2, g_pre, w_gu, w_down, g_post, *, tm=512, tf=256):
    m, d = x2.shape
    f = w_down.shape[0]
    n = f // tf
    wgu = w_gu.astype(BF).reshape(d, 2, n, tf).transpose(1, 2, 0, 3)
    wd = w_down.astype(BF).reshape(n, tf, d)
    row = pl.BlockSpec((tm, d), lambda i: (i, 0))
    vec = pl.BlockSpec((1, d), lambda i: (0, 0))
    return pl.pallas_call(
        functools.partial(_ffn_body, n_chunks=n),
        grid=(m // tm,),
        in_specs=[row, vec, _vmem(), _vmem(), _vmem(), vec],
        out_specs=row,
        out_shape=jax.ShapeDtypeStruct((m, d), F32),
        scratch_shapes=[pltpu.VMEM((tm, d), F32)],
        compiler_params=_params("parallel"),
        name="ffn",
    )(x2, g_pre.reshape(1, d), wgu[0], wgu[1], wd, g_post.reshape(1, d))


def _post_body(y1_ref, y2_ref, w1_ref, w2_ref, g_ref, x_ref, o_ref):
    y = _dot(y1_ref[...], w1_ref[...]) + _dot(y2_ref[...], w2_ref[...])
    o_ref[...] = x_ref[...] + _rms(y, g_ref[...])


def _post(y1, c1, y2, c2, w_out, g_post, x2, *, tm=1024):
    m, d = x2.shape
    half = w_out.shape[0] // 2
    w = w_out.astype(BF)
    row = pl.BlockSpec((tm, d), lambda i: (i, 0))
    return pl.pallas_call(
        _post_body,
        grid=(m // tm,),
        in_specs=[pl.BlockSpec((tm, half), lambda i: (i, c1)), pl.BlockSpec((tm, half), lambda i: (i, c2)),
                  _vmem(), _vmem(), pl.BlockSpec((1, d), lambda i: (0, 0)), row],
        out_specs=row,
        out_shape=jax.ShapeDtypeStruct((m, d), F32),
        compiler_params=_params("parallel"),
        name="mixer_out",
    )(y1, y2, w[:half], w[half:], g_post.reshape(1, d), x2)


def _even_pre_body(x_ref, gpre_ref, wm_ref, ws_ref, wst_ref, lng_ref, aws_ref, bst_ref, cw_ref,
                   alog_l_ref, dtb_l_ref, alog_c_ref, dtb_c_ref,
                   y_ref, q_ref, k_ref, v_ref, gate_ref, colp_ref, rowp_ref, zbuf_ref, *, ts):
    aw = A_GROUPS * LANE
    xn = _rms(x_ref[...], gpre_ref[...]).astype(BF)

    u = _gelu(_dot(xn, wm_ref[:, 0:aw]))
    v = _gelu(_dot(xn, wm_ref[:, aw:2 * aw]))
    r = lax.broadcasted_iota(jnp.int32, (A_CHUNK, A_CHUNK), 0)
    c = lax.broadcasted_iota(jnp.int32, (A_CHUNK, A_CHUNK), 1)
    for g in range(A_GROUPS):
        gs = slice(g * LANE, (g + 1) * LANE)
        vg = v[:, gs]
        d = vg - jnp.mean(vg, axis=-1, keepdims=True)
        var = jnp.mean(d * d, axis=-1, keepdims=True)
        vn = (d * lax.rsqrt(var + EPS) * lng_ref[:, gs]).astype(BF)
        w = jnp.where(r >= c, aws_ref[g], 0.0).astype(BF)
        bcol = bst_ref[:, g:g + 1]
        for ci in range(ts // A_CHUNK):
            rs = slice(ci * A_CHUNK, (ci + 1) * A_CHUNK)
            mixed = _dot(w, vn[rs]) + bcol
            y_ref[rs, gs] = (u[rs, gs] * mixed).astype(BF)

    cq = 3 * B_HEADS * LANE

    @pl.when(pl.program_id(1) == 0)
    def _():
        zbuf_ref[0:8, :] = jnp.zeros((8, cq), F32)

    zbuf_ref[8:8 + ts, :] = _dot(xn, wm_ref[:, 2 * aw:2 * aw + cq])
    for t in range(3 * B_HEADS):
        cs = slice(t * LANE, (t + 1) * LANE)
        acc = cw_ref[0:1, cs] * zbuf_ref[8 - B_CONV + 1:8 - B_CONV + 1 + ts, cs]
        for j in range(1, B_CONV):
            off = 8 - B_CONV + 1 + j
            acc = acc + cw_ref[j:j + 1, cs] * zbuf_ref[off:off + ts, cs]
        a = _silu(acc)
        if t < 2 * B_HEADS:
            a = a * lax.rsqrt(jnp.sum(a * a, axis=-1, keepdims=True) + EPS)
        if t < B_HEADS:
            q_ref[:, cs] = (a * (LANE ** -0.5)).astype(BF)
        elif t < 2 * B_HEADS:
            k_ref[:, slice((t - B_HEADS) * LANE, (t - B_HEADS + 1) * LANE)] = a.astype(BF)
        else:
            v_ref[:, slice((t - 2 * B_HEADS) * LANE, (t - 2 * B_HEADS + 1) * LANE)] = a.astype(BF)
    zbuf_ref[0:8, :] = zbuf_ref[ts:ts + 8, :]
    gate_ref[...] = _dot(xn, wm_ref[:, 2 * aw + cq:]).astype(BF)

    zs = _dot(xn, ws_ref[...])
    tri = _chunk_tri(ts)
    gcol = -jnp.exp(alog_l_ref[...]) * _softplus(zs + dtb_l_ref[...])
    gc = _dot(tri, gcol, HI)
    lane = lax.broadcasted_iota(jnp.int32, (ts, LANE), 1)
    colp_ref[...] = jnp.where(lane < B_HEADS, _sigmoid(zs), gc)
    zr = _dot_nt(wst_ref[...], xn)
    grow = -jnp.exp(alog_c_ref[...]) * _softplus(zr + dtb_c_ref[...])
    gcr = _dot_nt(grow, tri, HI)
    for ci in range(ts // CHUNK):
        rowp_ref[ci] = gcr[:, ci * CHUNK:(ci + 1) * CHUNK]


def _even_pre(x3, g_pre, w_in, a_ln_g, a_ws, a_bs, b_conv, a_log, dt_bias, *, ts=512):
    b, s, d = x3.shape
    aw = A_GROUPS * LANE
    nq = 3 * B_HEADS * LANE
    o_beta = 2 * aw + nq
    wm = jnp.concatenate([w_in[:, :o_beta], w_in[:, o_beta + 2 * B_HEADS:]], axis=1).astype(BF)
    ws = jnp.pad(w_in[:, o_beta:o_beta + 2 * B_HEADS], ((0, 0), (0, LANE - 2 * B_HEADS))).astype(BF)
    wst = jnp.pad(w_in[:, o_beta + B_HEADS:o_beta + 2 * B_HEADS].T, ((0, 8 - B_HEADS), (0, 0))).astype(BF)
    lane_pad = lambda p: jnp.pad(p, (B_HEADS, LANE - 2 * B_HEADS)).reshape(1, LANE)
    sub_pad = lambda p: jnp.pad(p, (0, 8 - B_HEADS)).reshape(8, 1)
    grid = (b, s // ts)
    tok = lambda w: pl.BlockSpec((None, ts, w), lambda bi, i: (bi, i, 0))
    vec = lambda w: pl.BlockSpec((1, w), lambda bi, i: (0, 0))
    nck = ts // CHUNK
    outs = pl.pallas_call(
        functools.partial(_even_pre_body, ts=ts),
        grid=grid,
        in_specs=[tok(d), vec(d), _vmem(), _vmem(), _vmem(), vec(aw), _vmem(), _vmem(), _vmem(),
                  vec(LANE), vec(LANE), _vmem(), _vmem()],
        out_specs=[tok(aw), tok(aw), tok(aw), tok(aw), tok(aw), tok(LANE),
                   pl.BlockSpec((None, nck, 8, CHUNK), lambda bi, i: (bi, i, 0, 0))],
        out_shape=[jax.ShapeDtypeStruct((b, s, aw), BF)] * 5
        + [jax.ShapeDtypeStruct((b, s, LANE), F32), jax.ShapeDtypeStruct((b, s // CHUNK, 8, CHUNK), F32)],
        scratch_shapes=[pltpu.VMEM((ts + 8, nq), F32)],
        compiler_params=_params("parallel", "arbitrary"),
        name="even_pre",
    )(x3, g_pre.reshape(1, d), wm, ws, wst, a_ln_g.reshape(1, aw), a_ws, a_bs.T, b_conv,
      lane_pad(a_log), lane_pad(dt_bias), sub_pad(a_log), sub_pad(dt_bias))
    return outs


def _gdn_wy_body(q_ref, k_ref, v_ref, colp_ref, rowp_ref,
                 u_ref, w_ref, qg_ref, attn_ref, kdt_ref, dec_ref, *, nck, cg):
    L = CHUNK
    r = lax.broadcasted_iota(jnp.int32, (L, L), 0)
    c = lax.broadcasted_iota(jnp.int32, (L, L), 1)
    causal = r >= c
    strict = r > c
    eye = jnp.where(r == c, 1.0, 0.0).astype(F32)
    shr = lax.shift_right_logical
    diag8 = jnp.where(shr(r, 3) == shr(c, 3), 1.0, 0.0).astype(F32)
    off = [jnp.where((shr(r, t + 1) == shr(c, t + 1)) & (shr(r, t) > shr(c, t)), 1.0, 0.0).astype(F32)
           for t in (3, 4, 5)]

    def mm(a, b):
        return _dot(a.astype(BF), b.astype(BF))

    def group(units):
        n = range(len(units))
        rows = [slice(ci * L, (ci + 1) * L) for ci, _ in units]
        hs = [slice(h * LANE, (h + 1) * LANE) for _, h in units]
        kh = [k_ref[rows[i], hs[i]] for i in n]
        khf = [kh[i].astype(F32) for i in n]
        beta = [colp_ref[rows[i], h:h + 1] for i, (_, h) in enumerate(units)]
        gc = [colp_ref[rows[i], B_HEADS + h:B_HEADS + h + 1] for i, (_, h) in enumerate(units)]
        decay = [jnp.exp(jnp.where(causal, gc[i] - rowp_ref[ci, h:h + 1, :], -jnp.inf))
                 for i, (ci, h) in enumerate(units)]
        kb = [khf[i] * beta[i] for i in n]
        kk = [_dot_nt(kb[i].astype(BF), kh[i]) for i in n]
        qk = [_dot_nt(q_ref[rows[i], hs[i]], kh[i]) for i in n]
        low = [jnp.where(strict, kk[i] * decay[i], 0.0) for i in n]
        for i, (_, h) in enumerate(units):
            attn_ref[h, rows[i], :] = jnp.where(causal, qk[i] * decay[i], 0.0).astype(BF)
        pw = [-(low[i] * diag8) for i in n]
        inv = [eye + pw[i] for i in n]
        for _ in range(2):
            pw = [mm(pw[i], pw[i]) for i in n]
            t = [mm(inv[i], pw[i]) for i in n]
            inv = [inv[i] + t[i] for i in n]
        for m in off:
            t = [mm(low[i] * m, inv[i]) for i in n]
            t = [mm(inv[i], t[i]) for i in n]
            inv = [inv[i] - t[i] for i in n]
        eg = [jnp.exp(gc[i]) for i in n]
        sol = [mm(inv[i], jnp.concatenate([v_ref[rows[i], hs[i]].astype(F32) * beta[i], kb[i] * eg[i]], axis=1))
               for i in n]
        for i, (ci, h) in enumerate(units):
            u_ref[rows[i], hs[i]] = sol[i][:, :LANE]
            w_ref[rows[i], hs[i]] = sol[i][:, LANE:].astype(BF)
            qg_ref[rows[i], hs[i]] = (q_ref[rows[i], hs[i]].astype(F32) * eg[i]).astype(BF)
            gl = gc[i][L - 1:L, :]
            kd = khf[i] * jnp.exp(gl - gc[i])
            kdt = jnp.concatenate([kd, jnp.zeros_like(kd)], axis=0).T
            kdt_ref[ci, h] = kdt[:, :L].astype(BF)
            dec_ref[ci, h:h + 1, :] = jnp.broadcast_to(jnp.exp(gl), (1, LANE))

    for c0 in range(0, nck, cg):
        group([(ci, h) for ci in range(c0, c0 + cg) for h in range(B_HEADS)])


def _gdn_wy(q, k, v, colp, rowp, *, ts=512, cg=4):
    b, s, w = q.shape
    nck = ts // CHUNK
    grid = (b, s // ts)
    tok = lambda n: pl.BlockSpec((None, ts, n), lambda bi, i: (bi, i, 0))
    return pl.pallas_call(
        functools.partial(_gdn_wy_body, nck=nck, cg=cg),
        grid=grid,
        in_specs=[tok(w), tok(w), tok(w), tok(LANE),
                  pl.BlockSpec((None, nck, 8, CHUNK), lambda bi, i: (bi, i, 0, 0))],
        out_specs=[tok(w), tok(w), tok(w),
                   pl.BlockSpec((None, B_HEADS, ts, CHUNK), lambda bi, i: (bi, 0, i, 0)),
                   pl.BlockSpec((None, nck, B_HEADS, LANE, CHUNK), lambda bi, i: (bi, i, 0, 0, 0)),
                   pl.BlockSpec((None, nck, B_HEADS, LANE), lambda bi, i: (bi, i, 0, 0))],
        out_shape=[jax.ShapeDtypeStruct((b, s, w), F32), jax.ShapeDtypeStruct((b, s, w), BF),
                   jax.ShapeDtypeStruct((b, s, w), BF),
                   jax.ShapeDtypeStruct((b, B_HEADS, s, CHUNK), BF),
                   jax.ShapeDtypeStruct((b, s // CHUNK, B_HEADS, LANE, CHUNK), BF),
                   jax.ShapeDtypeStruct((b, s // CHUNK, B_HEADS, LANE), F32)],
        compiler_params=_params("parallel", "parallel"),
        name="gdn_wy",
    )(q, k, v, colp, rowp)


def _gdn_scan_body(u_ref, w_ref, qg_ref, attn_ref, kdt_ref, dec_ref, gate_ref, ong_ref,
                   y_ref, s_ref, *, nck, bb):
    L = CHUNK

    @pl.when(pl.program_id(1) == 0)
    def _():
        s_ref[...] = jnp.zeros_like(s_ref)

    ch = [(bi, h, slice(h * LANE, (h + 1) * LANE)) for bi in range(bb) for h in range(B_HEADS)]
    for ci in range(nck):
        rows = slice(ci * L, (ci + 1) * L)
        sb = [s_ref[bi, h].astype(BF) for bi, h, _ in ch]
        ws = [_dot(w_ref[bi, rows, hs], sb[i]) for i, (bi, _, hs) in enumerate(ch)]
        qs = [_dot(qg_ref[bi, rows, hs], sb[i]) for i, (bi, _, hs) in enumerate(ch)]
        vb = [(u_ref[bi, rows, hs] - ws[i]).astype(BF) for i, (bi, _, hs) in enumerate(ch)]
        upd = [_dot(kdt_ref[bi, ci, h], vb[i]) for i, (bi, h, _) in enumerate(ch)]
        av = [_dot(attn_ref[bi, h, rows, :], vb[i]) for i, (bi, h, _) in enumerate(ch)]
        for i, (bi, h, hs) in enumerate(ch):
            s_ref[bi, h] = s_ref[bi, h] * dec_ref[bi, ci, h:h + 1, :] + upd[i]
        for i, (bi, h, hs) in enumerate(ch):
            on = _rms(qs[i] + av[i], ong_ref[...])
            y_ref[bi, rows, hs] = (on * _silu(gate_ref[bi, rows, hs].astype(F32))).astype(BF)


def _gdn_scan(u, w, qg, attn, kdt, dec, gate, onorm_g, *, ts=256, bb=8):
    b, s, wd = u.shape
    bb = min(bb, b)
    nck = ts // CHUNK
    tok = lambda n: pl.BlockSpec((bb, ts, n), lambda bi, i: (bi, i, 0))
    return pl.pallas_call(
        functools.partial(_gdn_scan_body, nck=nck, bb=bb),
        grid=(b // bb, s // ts),
        in_specs=[tok(wd), tok(wd), tok(wd),
                  pl.BlockSpec((bb, B_HEADS, ts, CHUNK), lambda bi, i: (bi, 0, i, 0)),
                  pl.BlockSpec((bb, nck, B_HEADS, LANE, CHUNK), lambda bi, i: (bi, i, 0, 0, 0)),
                  pl.BlockSpec((bb, nck, B_HEADS, LANE), lambda bi, i: (bi, i, 0, 0)),
                  tok(wd), pl.BlockSpec((1, LANE), lambda bi, i: (0, 0))],
        out_specs=tok(wd),
        out_shape=jax.ShapeDtypeStruct((b, s, wd), BF),
        scratch_shapes=[pltpu.VMEM((bb, B_HEADS, LANE, LANE), F32)],
        compiler_params=_params("parallel", "arbitrary"),
        name="gdn_scan",
    )(u, w, qg, attn, kdt, dec, gate, onorm_g.reshape(1, LANE))


def _odd_pre_body(x_ref, gpre_ref, wm_ref, ws_ref, wst_ref, bl_ref, bc_ref,
                  q_ref, k_ref, v_ref, og_ref, colp_ref, rowp_ref, *, ts):
    nqk = C_HEADS * C_DQK
    nv = C_HEADS * LANE
    xn = _rms(x_ref[...], gpre_ref[...]).astype(BF)
    q_ref[...] = _dot(xn, wm_ref[:, 0:nqk]).astype(BF)
    k_ref[...] = (_dot(xn, wm_ref[:, nqk:2 * nqk]) * (C_DQK ** -0.5)).astype(BF)
    v_ref[...] = _dot(xn, wm_ref[:, 2 * nqk:2 * nqk + nv]).astype(BF)
    og_ref[...] = _dot(xn, wm_ref[:, 2 * nqk + nv:]).astype(BF)

    tri = _chunk_tri(ts)
    cap = GATE_CAP * jnp.tanh((_dot(xn, ws_ref[...]) + bl_ref[...]) * (1.0 / GATE_CAP))
    bcum = _dot(tri, -_softplus(-cap), HI)
    lane = lax.broadcasted_iota(jnp.int32, (ts, LANE), 1)
    colp_ref[...] = jnp.where(lane < C_HEADS, cap, bcum)
    capr = GATE_CAP * jnp.tanh((_dot_nt(wst_ref[...], xn) + bc_ref[...]) * (1.0 / GATE_CAP))
    bcr = _dot_nt(-_softplus(-capr), tri, HI)
    sub = lax.broadcasted_iota(jnp.int32, (2 * C_HEADS, ts), 0)
    rp = jnp.where(sub < C_HEADS, capr, bcr)
    for ci in range(ts // CHUNK):
        rowp_ref[ci] = rp[:, ci * CHUNK:(ci + 1) * CHUNK]


def _odd_pre(x3, g_pre, w_in, b_i, b_f, *, ts=512):
    b, s, d = x3.shape
    nqk = C_HEADS * C_DQK
    nv = C_HEADS * LANE
    o_g = 2 * nqk + nv
    wm = jnp.concatenate([w_in[:, :o_g], w_in[:, o_g + 2 * C_HEADS:]], axis=1).astype(BF)
    wsm = w_in[:, o_g:o_g + 2 * C_HEADS]
    ws = jnp.pad(wsm, ((0, 0), (0, LANE - 2 * C_HEADS))).astype(BF)
    wst = wsm.T.astype(BF)
    bias = jnp.concatenate([b_i, b_f])
    tok = lambda n: pl.BlockSpec((None, ts, n), lambda bi, i: (bi, i, 0))
    vec = lambda n: pl.BlockSpec((1, n), lambda bi, i: (0, 0))
    nck = ts // CHUNK
    return pl.pallas_call(
        functools.partial(_odd_pre_body, ts=ts),
        grid=(b, s // ts),
        in_specs=[tok(d), vec(d), _vmem(), _vmem(), _vmem(), vec(LANE), _vmem()],
        out_specs=[tok(nqk), tok(nqk), tok(nv), tok(nv), tok(LANE),
                   pl.BlockSpec((None, nck, 2 * C_HEADS, CHUNK), lambda bi, i: (bi, i, 0, 0))],
        out_shape=[jax.ShapeDtypeStruct((b, s, nqk), BF), jax.ShapeDtypeStruct((b, s, nqk), BF),
                   jax.ShapeDtypeStruct((b, s, nv), BF), jax.ShapeDtypeStruct((b, s, nv), BF),
                   jax.ShapeDtypeStruct((b, s, LANE), F32),
                   jax.ShapeDtypeStruct((b, s // CHUNK, 2 * C_HEADS, CHUNK), F32)],
        compiler_params=_params("parallel", "parallel"),
        name="odd_pre",
    )(x3, g_pre.reshape(1, d), wm, ws, wst,
      jnp.pad(bias, (0, LANE - 2 * C_HEADS)).reshape(1, LANE), bias.reshape(2 * C_HEADS, 1))


def _mlstm_local_body(q_ref, k_ref, colp_ref, rowp_ref,
                      pk_ref, qs_ref, en_ref, se_ref, m_ref, *, nck):
    L = CHUNK
    r = lax.broadcasted_iota(jnp.int32, (L, L), 0)
    c = lax.broadcasted_iota(jnp.int32, (L, L), 1)
    causal = r >= c
    lane = lax.broadcasted_iota(jnp.int32, (L, LANE), 1)

    @pl.when(pl.program_id(1) == 0)
    def _():
        m_ref[...] = jnp.zeros_like(m_ref)

    heads = range(C_HEADS)
    hmask = [(lane >= (h % 2) * C_DQK) & (lane < (h % 2 + 1) * C_DQK) for h in heads]
    ms = [m_ref[h:h + 1, 0:1] for h in heads]
    for ci in range(nck):
        rows = slice(ci * L, (ci + 1) * L)
        colp = colp_ref[rows, :]
        rowp = rowp_ref[ci]
        ps = [slice((h // 2) * LANE, (h // 2 + 1) * LANE) for h in heads]
        qk = [_dot_nt(jnp.where(hmask[h], q_ref[rows, ps[h]], jnp.zeros((L, LANE), BF)), k_ref[rows, ps[h]])
              for h in heads]
        lic = [colp[:, h:h + 1] for h in heads]
        bc = [colp[:, C_HEADS + h:C_HEADS + h + 1] for h in heads]
        bl = [bc[h][L - 1:L, :] for h in heads]
        dlog = [jnp.where(causal, bc[h] - rowp[C_HEADS + h:C_HEADS + h + 1, :] + rowp[h:h + 1, :], -jnp.inf)
                for h in heads]
        dmax = [jnp.max(dlog[h], axis=-1, keepdims=True) for h in heads]
        wmax = [jnp.max(bl[h] - rowp[C_HEADS + h:C_HEADS + h + 1, :] + rowp[h:h + 1, :], axis=-1, keepdims=True)
                for h in heads]
        m_old = list(ms)
        ms = [jnp.maximum(bl[h] + m_old[h], wmax[h]) for h in heads]
        inter = [bc[h] + m_old[h] for h in heads]
        mt = [jnp.maximum(inter[h], dmax[h]) for h in heads]
        en_tile = jnp.zeros((L, LANE), F32)
        for h in heads:
            pk_ref[ci, h, 0:L, :] = (jnp.exp(dlog[h] - mt[h]) * qk[h]).astype(BF)
            en_tile = jnp.where(lane == h, jnp.exp(-mt[h]), en_tile)
            se_ref[ci, h:h + 1, :] = jnp.broadcast_to(jnp.exp(bl[h] + m_old[h] - ms[h]), (1, LANE))
        en_ref[rows, :] = en_tile
        for p in range(C_HEADS // 2):
            h0, h1 = 2 * p, 2 * p + 1
            qp = q_ref[rows, ps[h0]].astype(F32)
            kpf = k_ref[rows, ps[h0]].astype(F32)
            qs = (qp * jnp.where(hmask[h0], jnp.exp(inter[h0] - mt[h0]),
                                 jnp.exp(inter[h1] - mt[h1]))).astype(BF)
            zero = jnp.zeros_like(qs)
            qs_ref[ci, p, 0:L, :] = jnp.where(hmask[h0], qs, zero)
            qs_ref[ci, p, L:2 * L, :] = jnp.where(hmask[h1], qs, zero)
            kw = kpf * jnp.where(hmask[h0], jnp.exp(bl[h0] - bc[h0] + lic[h0] - ms[h0]),
                                 jnp.exp(bl[h1] - bc[h1] + lic[h1] - ms[h1]))
            kwt = jnp.concatenate([kw, jnp.zeros_like(kw)], axis=0).T[:, :L].astype(BF)
            pk_ref[ci, h0, L:2 * L, :] = kwt[0:C_DQK]
            pk_ref[ci, h1, L:2 * L, :] = kwt[C_DQK:2 * C_DQK]
    for h in heads:
        m_ref[h:h + 1, :] = jnp.broadcast_to(ms[h], (1, LANE))


def _mlstm_local(q, k, colp, rowp, *, ts=512):
    b, s, nqk = q.shape
    nck = ts // CHUNK
    npair = C_HEADS // 2
    tok = lambda n: pl.BlockSpec((None, ts, n), lambda bi, i: (bi, i, 0))
    return pl.pallas_call(
        functools.partial(_mlstm_local_body, nck=nck),
        grid=(b, s // ts),
        in_specs=[tok(nqk), tok(nqk), tok(LANE),
                  pl.BlockSpec((None, nck, 2 * C_HEADS, CHUNK), lambda bi, i: (bi, i, 0, 0))],
        out_specs=[pl.BlockSpec((None, nck, C_HEADS, 2 * CHUNK, CHUNK), lambda bi, i: (bi, i, 0, 0, 0)),
                   pl.BlockSpec((None, nck, npair, 2 * CHUNK, LANE), lambda bi, i: (bi, i, 0, 0, 0)),
                   tok(LANE),
                   pl.BlockSpec((None, nck, C_HEADS, LANE), lambda bi, i: (bi, i, 0, 0))],
        out_shape=[jax.ShapeDtypeStruct((b, s // CHUNK, C_HEADS, 2 * CHUNK, CHUNK), BF),
                   jax.ShapeDtypeStruct((b, s // CHUNK, npair, 2 * CHUNK, LANE), BF),
                   jax.ShapeDtypeStruct((b, s, LANE), F32),
                   jax.ShapeDtypeStruct((b, s // CHUNK, C_HEADS, LANE), F32)],
        scratch_shapes=[pltpu.VMEM((C_HEADS, LANE), F32)],
        compiler_params=_params("parallel", "arbitrary"),
        name="mlstm_local",
    )(q, k, colp, rowp)


def _mlstm_scan_body(pk_ref, qs_ref, en_ref, se_ref, v_ref, og_ref, ong_ref,
                     y_ref, c_ref, *, nck, bb):
    L = CHUNK
    ones = jnp.ones((L, LANE), BF)

    @pl.when(pl.program_id(1) == 0)
    def _():
        c_ref[...] = jnp.zeros_like(c_ref)

    ch = [(bi, h) for bi in range(bb) for h in range(C_HEADS)]
    pairs = [(bi, p) for bi in range(bb) for p in range(C_HEADS // 2)]
    hsl = lambda h: slice(h * LANE, (h + 1) * LANE)
    srl = lambda h: slice((h % 2) * C_DQK, (h % 2 + 1) * C_DQK)
    for ci in range(nck):
        rows = slice(ci * L, (ci + 1) * L)
        qc = {(bi, p): _dot(qs_ref[bi, ci, p], c_ref[bi, p].astype(BF)) for bi, p in pairs}
        pv = [_dot(pk_ref[bi, ci, h], jnp.concatenate([v_ref[bi, rows, hsl(h)], ones], axis=1))
              for bi, h in ch]
        for i, (bi, h) in enumerate(ch):
            se = se_ref[bi, ci, h:h + 1, :]
            c_ref[bi, h // 2, srl(h), :] = (c_ref[bi, h // 2, srl(h), :] * jnp.concatenate([se, se], axis=1)
                                            + pv[i][L:2 * L])
        res = [qc[(bi, h // 2)][srl(h)] + pv[i][0:L] for i, (bi, h) in enumerate(ch)]
        den = [jnp.maximum(jnp.abs(res[i][:, LANE:]), en_ref[bi, rows, h:h + 1]) for i, (bi, h) in enumerate(ch)]
        hh = [_sigmoid(og_ref[bi, rows, hsl(h)].astype(F32)) * (res[i][:, :LANE] / den[i])
              for i, (bi, h) in enumerate(ch)]
        for i, (bi, h) in enumerate(ch):
            y_ref[bi, rows, hsl(h)] = _rms(hh[i], ong_ref[:, hsl(h)]).astype(BF)


def _mlstm_scan(pk, qs, en, se, v, og, onorm_g, *, ts=256, bb=4):
    b, s, nv = v.shape
    bb = min(bb, b)
    nck = ts // CHUNK
    npair = C_HEADS // 2
    tok = lambda n: pl.BlockSpec((bb, ts, n), lambda bi, i: (bi, i, 0))
    return pl.pallas_call(
        functools.partial(_mlstm_scan_body, nck=nck, bb=bb),
        grid=(b // bb, s // ts),
        in_specs=[pl.BlockSpec((bb, nck, C_HEADS, 2 * CHUNK, CHUNK), lambda bi, i: (bi, i, 0, 0, 0)),
                  pl.BlockSpec((bb, nck, npair, 2 * CHUNK, LANE), lambda bi, i: (bi, i, 0, 0, 0)),
                  tok(LANE),
                  pl.BlockSpec((bb, nck, C_HEADS, LANE), lambda bi, i: (bi, i, 0, 0)),
                  tok(nv), tok(nv), pl.BlockSpec((1, nv), lambda bi, i: (0, 0))],
        out_specs=tok(nv),
        out_shape=jax.ShapeDtypeStruct((b, s, nv), BF),
        scratch_shapes=[pltpu.VMEM((bb, npair, LANE, 2 * LANE), F32)],
        compiler_params=_params("parallel", "arbitrary"),
        name="mlstm_scan",
    )(pk, qs, en, se, v, og, onorm_g.reshape(1, nv))


def kernel(x, norm_g, ffn1_gu, ffn1_down, ffn2_gu, ffn2_down, ev_w_in, ev_w_out, ev_a_ln_g, ev_a_ws, ev_a_bs, ev_b_conv, ev_b_a_log, ev_b_dt_bias, ev_b_onorm_g, od_w_in, od_w_out, od_b_i, od_b_f, od_onorm_g):
    b, s, d = x.shape
    x2 = x.reshape(b * s, d)
    depth = norm_g.shape[0]
    for layer in range(depth):
        ng = norm_g[layer]
        j = layer // 2
        x2 = _ffn(x2, ng[0], ffn1_gu[layer], ffn1_down[layer], ng[1])
        x3 = x2.reshape(b, s, d)
        if layer % 2 == 0:
            ya, q, k, v, gate, colp, rowp = _even_pre(
                x3, ng[2], ev_w_in[j], ev_a_ln_g[j], ev_a_ws[j], ev_a_bs[j], ev_b_conv[j],
                ev_b_a_log[j], ev_b_dt_bias[j])
            u, w, qg, attn, kdt, dec = _gdn_wy(q, k, v, colp, rowp)
            yb = _gdn_scan(u, w, qg, attn, kdt, dec, gate, ev_b_onorm_g[j])
            x2 = _post(ya.reshape(b * s, -1), 0, yb.reshape(b * s, -1), 0, ev_w_out[j], ng[3], x2)
        else:
            q, k, v, og, colp, rowp = _odd_pre(x3, ng[2], od_w_in[j], od_b_i[j], od_b_f[j])
            pk, qs, en, se = _mlstm_local(q, k, colp, rowp)
            y = _mlstm_scan(pk, qs, en, se, v, og, od_onorm_g[j]).reshape(b * s, -1)
            x2 = _post(y, 0, y, 1, od_w_out[j], ng[3], x2)
        x2 = _ffn(x2, ng[4], ffn2_gu[layer], ffn2_down[layer], ng[5])
    return x2.reshape(b, s, d)
```

```python
import functools

import jax
import jax.numpy as jnp
from jax import lax
from jax.experimental import pallas as pl
from jax.experimental.pallas import tpu as pltpu

F32 = jnp.float32
BF = jnp.bfloat16
HI = lax.Precision.HIGHEST

EPS = 1e-6
LANE = 128
CHUNK = 64
A_GROUPS, A_CHUNK = 4, 128
B_HEADS, B_CONV = 4, 4
C_HEADS, C_DQK = 8, 64
GATE_CAP = 15.0
VMEM_LIMIT = 56 * 1024 * 1024


def _dot(a, b, prec=None):
    return jnp.dot(a, b, preferred_element_type=F32, precision=prec)


def _dot_nt(a, b, prec=None):
    return lax.dot_general(a, b, (((1,), (1,)), ((), ())), preferred_element_type=F32, precision=prec)


def _rms(xf, g):
    return xf * lax.rsqrt(jnp.mean(xf * xf, axis=-1, keepdims=True) + EPS) * g


def _sigmoid(x):
    return 1.0 / (1.0 + jnp.exp(-x))


def _silu(x):
    return x * _sigmoid(x)


def _gelu(x):
    return 0.5 * x * (1.0 + jnp.tanh(0.7978845608028654 * (x + 0.044715 * (x * x * x))))


def _softplus(x):
    return jnp.maximum(x, 0.0) + jnp.log1p(jnp.exp(-jnp.abs(x)))


def _chunk_tri(n):
    r = lax.broadcasted_iota(jnp.int32, (n, n), 0)
    c = lax.broadcasted_iota(jnp.int32, (n, n), 1)
    same = lax.shift_right_logical(r, 6) == lax.shift_right_logical(c, 6)
    return jnp.where(same, jnp.where(c <= r, 1.0, 0.0), 0.0).astype(BF)


def _split3(x):
    p1 = x.astype(BF).astype(F32)
    r1 = x - p1
    p2 = r1.astype(BF).astype(F32)
    return p1, p2, (r1 - p2).astype(BF).astype(F32)


SPLIT_LANES = 16


def _chunk_cumsum_cols(tri, x):
    lane = lax.broadcasted_iota(jnp.int32, x.shape, 1)
    p1, p2, p3 = _split3(jnp.where(lane < SPLIT_LANES, x, 0.0))
    packed = p1 + pltpu.roll(p2, SPLIT_LANES, axis=1) + pltpu.roll(p3, 2 * SPLIT_LANES, axis=1)
    res = _dot(tri, packed.astype(BF))
    return res + pltpu.roll(res, LANE - SPLIT_LANES, axis=1) + pltpu.roll(res, LANE - 2 * SPLIT_LANES, axis=1)


def _chunk_cumsum_rows(tri, x):
    r = x.shape[0]
    p1, p2, p3 = _split3(x)
    res = _dot_nt(jnp.concatenate([p1, p2, p3, jnp.zeros_like(p1)], axis=0).astype(BF), tri)
    return res[0:r] + res[r:2 * r] + res[2 * r:3 * r]


def _params(*sem):
    return pltpu.CompilerParams(dimension_semantics=sem, vmem_limit_bytes=VMEM_LIMIT)


def _vmem():
    return pl.BlockSpec(memory_space=pltpu.VMEM)


def _ffn_body(x_ref, gpre_ref, wgu_ref, wd_ref, gpost_ref, o_ref, acc_ref, *, tf):
    x = x_ref[...]
    xn = _rms(x, gpre_ref[...]).astype(BF)
    f = wd_ref.shape[0]

    def gate_up(j):
        return (_dot(xn, wgu_ref[:, j * tf:(j + 1) * tf]), _dot(xn, wgu_ref[:, f + j * tf:f + (j + 1) * tf]))

    n_chunks = f // tf
    gu = gate_up(0)
    for j in range(n_chunks):
        g, u = gu
        if j + 1 < n_chunks:
            gu = gate_up(j + 1)
        d = _dot((_silu(g) * u).astype(BF), wd_ref[j * tf:(j + 1) * tf, :])
        if j == 0:
            acc_ref[...] = d
        else:
            acc_ref[...] += d
    o_ref[...] = x + _rms(acc_ref[...], 0.5 * gpost_ref[...])


def _ffn(x2, g_pre, w_gu, w_down, g_post, *, tm=512, tf=256):
    m, d = x2.shape
    row = pl.BlockSpec((tm, d), lambda i: (i, 0))
    vec = pl.BlockSpec((1, d), lambda i: (0, 0))
    return pl.pallas_call(
        functools.partial(_ffn_body, tf=tf),
        grid=(m // tm,),
        in_specs=[row, vec, _vmem(), _vmem(), vec],
        out_specs=row,
        out_shape=jax.ShapeDtypeStruct((m, d), F32),
        scratch_shapes=[pltpu.VMEM((tm, d), F32)],
        compiler_params=_params("parallel"),
        name="ffn",
    )(x2, g_pre.reshape(1, d), w_gu.astype(BF), w_down.astype(BF), g_post.reshape(1, d))


def _post_body(y1_ref, y2_ref, w1_ref, w2_ref, g_ref, x_ref, o_ref):
    y = _dot(y1_ref[...], w1_ref[...]) + _dot(y2_ref[...], w2_ref[...])
    o_ref[...] = x_ref[...] + _rms(y, g_ref[...])


def _post(y1, c1, y2, c2, w_out, g_post, x2, *, tm=1024):
    m, d = x2.shape
    half = w_out.shape[0] // 2
    w = w_out.astype(BF)
    row = pl.BlockSpec((tm, d), lambda i: (i, 0))
    return pl.pallas_call(
        _post_body,
        grid=(m // tm,),
        in_specs=[pl.BlockSpec((tm, half), lambda i: (i, c1)), pl.BlockSpec((tm, half), lambda i: (i, c2)),
                  _vmem(), _vmem(), pl.BlockSpec((1, d), lambda i: (0, 0)), row],
        out_specs=row,
        out_shape=jax.ShapeDtypeStruct((m, d), F32),
        compiler_params=_params("parallel"),
        name="mixer_out",
    )(y1, y2, w[:half], w[half:], g_post.reshape(1, d), x2)


def _even_pre_body(x_ref, gpre_ref, wm_ref, ws_ref, wst_ref, lng_ref, aws_ref, bst_ref, cw_ref,
                   alog_l_ref, dtb_l_ref, alog_c_ref, dtb_c_ref,
                   y_ref, q_ref, k_ref, v_ref, gate_ref, colp_ref, rowp_ref, zbuf_ref, *, ts):
    aw = A_GROUPS * LANE
    xn = _rms(x_ref[...], gpre_ref[...]).astype(BF)

    u = _gelu(_dot(xn, wm_ref[:, 0:aw]))
    v = _gelu(_dot(xn, wm_ref[:, aw:2 * aw]))
    r = lax.broadcasted_iota(jnp.int32, (A_CHUNK, A_CHUNK), 0)
    c = lax.broadcasted_iota(jnp.int32, (A_CHUNK, A_CHUNK), 1)
    for g in range(A_GROUPS):
        gs = slice(g * LANE, (g + 1) * LANE)
        vg = v[:, gs]
        d = vg - jnp.mean(vg, axis=-1, keepdims=True)
        var = jnp.mean(d * d, axis=-1, keepdims=True)
        vn = (d * lax.rsqrt(var + EPS) * lng_ref[:, gs]).astype(BF)
        w = jnp.where(r >= c, aws_ref[g], 0.0).astype(BF)
        bcol = bst_ref[:, g:g + 1]
        for ci in range(ts // A_CHUNK):
            rs = slice(ci * A_CHUNK, (ci + 1) * A_CHUNK)
            mixed = _dot(w, vn[rs]) + bcol
            y_ref[rs, gs] = (u[rs, gs] * mixed).astype(BF)

    cq = 3 * B_HEADS * LANE

    @pl.when(pl.program_id(1) == 0)
    def _():
        zbuf_ref[0:8, :] = jnp.zeros((8, cq), F32)

    zbuf_ref[8:8 + ts, :] = _dot(xn, wm_ref[:, 2 * aw:2 * aw + cq])
    for t in range(3 * B_HEADS):
        cs = slice(t * LANE, (t + 1) * LANE)
        zz = zbuf_ref[:, cs]
        acc = cw_ref[0:1, cs] * zz
        for j in range(1, B_CONV):
            acc = pltpu.roll(acc, 1, axis=0) + cw_ref[j:j + 1, cs] * zz
        a = _silu(acc[8:])
        if t < 2 * B_HEADS:
            a = a * lax.rsqrt(jnp.sum(a * a, axis=-1, keepdims=True) + EPS)
        if t < B_HEADS:
            q_ref[:, cs] = (a * (LANE ** -0.5)).astype(BF)
        elif t < 2 * B_HEADS:
            k_ref[:, slice((t - B_HEADS) * LANE, (t - B_HEADS + 1) * LANE)] = a.astype(BF)
        else:
            v_ref[:, slice((t - 2 * B_HEADS) * LANE, (t - 2 * B_HEADS + 1) * LANE)] = a.astype(BF)
    zbuf_ref[0:8, :] = zbuf_ref[ts:ts + 8, :]
    gate_ref[...] = _dot(xn, wm_ref[:, 2 * aw + cq:]).astype(BF)

    zs = _dot(xn, ws_ref[...])
    tri = _chunk_tri(ts)
    gcol = -jnp.exp(alog_l_ref[...]) * _softplus(zs + dtb_l_ref[...])
    gc = _chunk_cumsum_cols(tri, gcol)
    lane = lax.broadcasted_iota(jnp.int32, (ts, LANE), 1)
    colp_ref[...] = jnp.where(lane < B_HEADS, _sigmoid(zs), gc)
    zr = _dot_nt(wst_ref[...], xn)
    grow = -jnp.exp(alog_c_ref[...]) * _softplus(zr + dtb_c_ref[...])
    gcr = _chunk_cumsum_rows(tri, grow)
    for ci in range(ts // CHUNK):
        rowp_ref[ci] = gcr[:, ci * CHUNK:(ci + 1) * CHUNK]


def _even_pre(x3, g_pre, w_in, a_ln_g, a_ws, a_bs, b_conv, a_log, dt_bias, *, ts=512):
    b, s, d = x3.shape
    aw = A_GROUPS * LANE
    nq = 3 * B_HEADS * LANE
    o_beta = 2 * aw + nq
    wm = jnp.concatenate([w_in[:, :o_beta], w_in[:, o_beta + 2 * B_HEADS:]], axis=1).astype(BF)
    ws = jnp.pad(w_in[:, o_beta:o_beta + 2 * B_HEADS], ((0, 0), (0, LANE - 2 * B_HEADS))).astype(BF)
    wst = jnp.pad(w_in[:, o_beta + B_HEADS:o_beta + 2 * B_HEADS].T, ((0, 8 - B_HEADS), (0, 0))).astype(BF)
    lane_pad = lambda p: jnp.pad(p, (B_HEADS, LANE - 2 * B_HEADS)).reshape(1, LANE)
    sub_pad = lambda p: jnp.pad(p, (0, 8 - B_HEADS)).reshape(8, 1)
    grid = (b, s // ts)
    tok = lambda w: pl.BlockSpec((None, ts, w), lambda bi, i: (bi, i, 0))
    vec = lambda w: pl.BlockSpec((1, w), lambda bi, i: (0, 0))
    nck = ts // CHUNK
    outs = pl.pallas_call(
        functools.partial(_even_pre_body, ts=ts),
        grid=grid,
        in_specs=[tok(d), vec(d), _vmem(), _vmem(), _vmem(), vec(aw), _vmem(), _vmem(), _vmem(),
                  vec(LANE), vec(LANE), _vmem(), _vmem()],
        out_specs=[tok(aw), tok(aw), tok(aw), tok(aw), tok(aw), tok(LANE),
                   pl.BlockSpec((None, nck, 8, CHUNK), lambda bi, i: (bi, i, 0, 0))],
        out_shape=[jax.ShapeDtypeStruct((b, s, aw), BF)] * 5
        + [jax.ShapeDtypeStruct((b, s, LANE), F32), jax.ShapeDtypeStruct((b, s // CHUNK, 8, CHUNK), F32)],
        scratch_shapes=[pltpu.VMEM((ts + 8, nq), F32)],
        compiler_params=_params("parallel", "arbitrary"),
        name="even_pre",
    )(x3, g_pre.reshape(1, d), wm, ws, wst, a_ln_g.reshape(1, aw), a_ws, a_bs.T, b_conv,
      lane_pad(a_log), lane_pad(dt_bias), sub_pad(a_log), sub_pad(dt_bias))
    return outs


def _gdn_wy_body(q_ref, k_ref, v_ref, colp_ref, rowp_ref,
                 u_ref, w_ref, qg_ref, attn_ref, kdt_ref, dec_ref, *, nck, cg):
    L = CHUNK
    r = lax.broadcasted_iota(jnp.int32, (L, L), 0)
    c = lax.broadcasted_iota(jnp.int32, (L, L), 1)
    causal = r >= c
    strict = r > c
    eye = jnp.where(r == c, 1.0, 0.0).astype(F32)
    shr = lax.shift_right_logical
    diag8 = jnp.where(shr(r, 3) == shr(c, 3), 1.0, 0.0).astype(F32)
    off = [jnp.where((shr(r, t + 1) == shr(c, t + 1)) & (shr(r, t) > shr(c, t)), 1.0, 0.0).astype(F32)
           for t in (3, 4, 5)]

    def mm(a, b):
        return _dot(a.astype(BF), b.astype(BF))

    def group(units):
        n = range(len(units))
        rows = [slice(ci * L, (ci + 1) * L) for ci, _ in units]
        hs = [slice(h * LANE, (h + 1) * LANE) for _, h in units]
        kh = [k_ref[rows[i], hs[i]] for i in n]
        khf = [kh[i].astype(F32) for i in n]
        beta = [colp_ref[rows[i], h:h + 1] for i, (_, h) in enumerate(units)]
        gc = [colp_ref[rows[i], B_HEADS + h:B_HEADS + h + 1] for i, (_, h) in enumerate(units)]
        decay = [jnp.exp(jnp.where(causal, gc[i] - rowp_ref[ci, h:h + 1, :], -jnp.inf))
                 for i, (ci, h) in enumerate(units)]
        kb = [khf[i] * beta[i] for i in n]
        kk = [_dot_nt(kb[i].astype(BF), kh[i]) for i in n]
        qk = [_dot_nt(q_ref[rows[i], hs[i]], kh[i]) for i in n]
        low = [jnp.where(strict, kk[i] * decay[i], 0.0) for i in n]
        for i, (_, h) in enumerate(units):
            attn_ref[h, rows[i], :] = jnp.where(causal, qk[i] * decay[i], 0.0).astype(BF)
        pw = [-(low[i] * diag8) for i in n]
        inv = [eye + pw[i] for i in n]
        for _ in range(2):
            pw = [mm(pw[i], pw[i]) for i in n]
            t = [mm(inv[i], pw[i]) for i in n]
            inv = [inv[i] + t[i] for i in n]
        for m in off:
            t = [mm(low[i] * m, inv[i]) for i in n]
            t = [mm(inv[i], t[i]) for i in n]
            inv = [inv[i] - t[i] for i in n]
        eg = [jnp.exp(gc[i]) for i in n]
        sol = [mm(inv[i], jnp.concatenate([v_ref[rows[i], hs[i]].astype(F32) * beta[i], kb[i] * eg[i]], axis=1))
               for i in n]
        for i, (ci, h) in enumerate(units):
            u_ref[rows[i], hs[i]] = sol[i][:, :LANE]
            w_ref[rows[i], hs[i]] = sol[i][:, LANE:].astype(BF)
            qg_ref[rows[i], hs[i]] = (q_ref[rows[i], hs[i]].astype(F32) * eg[i]).astype(BF)
            gl = gc[i][L - 1:L, :]
            kd = khf[i] * jnp.exp(gl - gc[i])
            kdt = jnp.concatenate([kd, jnp.zeros_like(kd)], axis=0).T
            kdt_ref[ci, h] = kdt[:, :L].astype(BF)
            dec_ref[ci, h:h + 1, :] = jnp.broadcast_to(jnp.exp(gl), (1, LANE))

    for c0 in range(0, nck, cg):
        group([(ci, h) for ci in range(c0, c0 + cg) for h in range(B_HEADS)])


def _gdn_wy(q, k, v, colp, rowp, *, ts=512, cg=4):
    b, s, w = q.shape
    nck = ts // CHUNK
    grid = (b, s // ts)
    tok = lambda n: pl.BlockSpec((None, ts, n), lambda bi, i: (bi, i, 0))
    return pl.pallas_call(
        functools.partial(_gdn_wy_body, nck=nck, cg=cg),
        grid=grid,
        in_specs=[tok(w), tok(w), tok(w), tok(LANE),
                  pl.BlockSpec((None, nck, 8, CHUNK), lambda bi, i: (bi, i, 0, 0))],
        out_specs=[tok(w), tok(w), tok(w),
                   pl.BlockSpec((None, B_HEADS, ts, CHUNK), lambda bi, i: (bi, 0, i, 0)),
                   pl.BlockSpec((None, nck, B_HEADS, LANE, CHUNK), lambda bi, i: (bi, i, 0, 0, 0)),
                   pl.BlockSpec((None, nck, B_HEADS, LANE), lambda bi, i: (bi, i, 0, 0))],
        out_shape=[jax.ShapeDtypeStruct((b, s, w), F32), jax.ShapeDtypeStruct((b, s, w), BF),
                   jax.ShapeDtypeStruct((b, s, w), BF),
                   jax.ShapeDtypeStruct((b, B_HEADS, s, CHUNK), BF),
                   jax.ShapeDtypeStruct((b, s // CHUNK, B_HEADS, LANE, CHUNK), BF),
                   jax.ShapeDtypeStruct((b, s // CHUNK, B_HEADS, LANE), F32)],
        compiler_params=_params("parallel", "parallel"),
        name="gdn_wy",
    )(q, k, v, colp, rowp)


def _gdn_scan_body(u_ref, w_ref, qg_ref, attn_ref, kdt_ref, dec_ref, gate_ref, ong_ref,
                   y_ref, s_ref, *, nck, bb):
    L = CHUNK

    @pl.when(pl.program_id(1) == 0)
    def _():
        s_ref[...] = jnp.zeros_like(s_ref)

    ch = [(bi, h, slice(h * LANE, (h + 1) * LANE)) for bi in range(bb) for h in range(B_HEADS)]
    for ci in range(nck):
        rows = slice(ci * L, (ci + 1) * L)
        sb = [s_ref[bi, h].astype(BF) for bi, h, _ in ch]
        ws = [_dot(w_ref[bi, rows, hs], sb[i]) for i, (bi, _, hs) in enumerate(ch)]
        qs = [_dot(qg_ref[bi, rows, hs], sb[i]) for i, (bi, _, hs) in enumerate(ch)]
        vb = [(u_ref[bi, rows, hs] - ws[i]).astype(BF) for i, (bi, _, hs) in enumerate(ch)]
        upd = [_dot(kdt_ref[bi, ci, h], vb[i]) for i, (bi, h, _) in enumerate(ch)]
        av = [_dot(attn_ref[bi, h, rows, :], vb[i]) for i, (bi, h, _) in enumerate(ch)]
        for i, (bi, h, hs) in enumerate(ch):
            s_ref[bi, h] = s_ref[bi, h] * dec_ref[bi, ci, h:h + 1, :] + upd[i]
        for i, (bi, h, hs) in enumerate(ch):
            on = _rms(qs[i] + av[i], ong_ref[...])
            y_ref[bi, rows, hs] = (on * _silu(gate_ref[bi, rows, hs].astype(F32))).astype(BF)


def _gdn_scan(u, w, qg, attn, kdt, dec, gate, onorm_g, *, ts=256, bb=8):
    b, s, wd = u.shape
    bb = min(bb, b)
    nck = ts // CHUNK
    tok = lambda n: pl.BlockSpec((bb, ts, n), lambda bi, i: (bi, i, 0))
    return pl.pallas_call(
        functools.partial(_gdn_scan_body, nck=nck, bb=bb),
        grid=(b // bb, s // ts),
        in_specs=[tok(wd), tok(wd), tok(wd),
                  pl.BlockSpec((bb, B_HEADS, ts, CHUNK), lambda bi, i: (bi, 0, i, 0)),
                  pl.BlockSpec((bb, nck, B_HEADS, LANE, CHUNK), lambda bi, i: (bi, i, 0, 0, 0)),
                  pl.BlockSpec((bb, nck, B_HEADS, LANE), lambda bi, i: (bi, i, 0, 0)),
                  tok(wd), pl.BlockSpec((1, LANE), lambda bi, i: (0, 0))],
        out_specs=tok(wd),
        out_shape=jax.ShapeDtypeStruct((b, s, wd), BF),
        scratch_shapes=[pltpu.VMEM((bb, B_HEADS, LANE, LANE), F32)],
        compiler_params=_params("parallel", "arbitrary"),
        name="gdn_scan",
    )(u, w, qg, attn, kdt, dec, gate, onorm_g.reshape(1, LANE))


def _odd_pre_body(x_ref, gpre_ref, wm_ref, ws_ref, wst_ref, bl_ref, bc_ref,
                  q_ref, k_ref, v_ref, og_ref, colp_ref, rowp_ref, *, ts):
    nqk = C_HEADS * C_DQK
    nv = C_HEADS * LANE
    xn = _rms(x_ref[...], gpre_ref[...]).astype(BF)
    q_ref[...] = _dot(xn, wm_ref[:, 0:nqk]).astype(BF)
    k_ref[...] = (_dot(xn, wm_ref[:, nqk:2 * nqk]) * (C_DQK ** -0.5)).astype(BF)
    v_ref[...] = _dot(xn, wm_ref[:, 2 * nqk:2 * nqk + nv]).astype(BF)
    og_ref[...] = _dot(xn, wm_ref[:, 2 * nqk + nv:]).astype(BF)

    tri = _chunk_tri(ts)
    cap = GATE_CAP * jnp.tanh((_dot(xn, ws_ref[...]) + bl_ref[...]) * (1.0 / GATE_CAP))
    bcum = _chunk_cumsum_cols(tri, -_softplus(-cap))
    lane = lax.broadcasted_iota(jnp.int32, (ts, LANE), 1)
    colp_ref[...] = jnp.where(lane < C_HEADS, cap, bcum)
    capr = GATE_CAP * jnp.tanh((_dot_nt(wst_ref[...], xn) + bc_ref[...]) * (1.0 / GATE_CAP))
    bcr = _chunk_cumsum_rows(tri, -_softplus(-capr))
    sub = lax.broadcasted_iota(jnp.int32, (2 * C_HEADS, ts), 0)
    rp = jnp.where(sub < C_HEADS, capr, bcr)
    for ci in range(ts // CHUNK):
        rowp_ref[ci] = rp[:, ci * CHUNK:(ci + 1) * CHUNK]


def _odd_pre(x3, g_pre, w_in, b_i, b_f, *, ts=512):
    b, s, d = x3.shape
    nqk = C_HEADS * C_DQK
    nv = C_HEADS * LANE
    o_g = 2 * nqk + nv
    wm = jnp.concatenate([w_in[:, :o_g], w_in[:, o_g + 2 * C_HEADS:]], axis=1).astype(BF)
    wsm = w_in[:, o_g:o_g + 2 * C_HEADS]
    ws = jnp.pad(wsm, ((0, 0), (0, LANE - 2 * C_HEADS))).astype(BF)
    wst = wsm.T.astype(BF)
    bias = jnp.concatenate([b_i, b_f])
    tok = lambda n: pl.BlockSpec((None, ts, n), lambda bi, i: (bi, i, 0))
    vec = lambda n: pl.BlockSpec((1, n), lambda bi, i: (0, 0))
    nck = ts // CHUNK
    return pl.pallas_call(
        functools.partial(_odd_pre_body, ts=ts),
        grid=(b, s // ts),
        in_specs=[tok(d), vec(d), _vmem(), _vmem(), _vmem(), vec(LANE), _vmem()],
        out_specs=[tok(nqk), tok(nqk), tok(nv), tok(nv), tok(LANE),
                   pl.BlockSpec((None, nck, 2 * C_HEADS, CHUNK), lambda bi, i: (bi, i, 0, 0))],
        out_shape=[jax.ShapeDtypeStruct((b, s, nqk), BF), jax.ShapeDtypeStruct((b, s, nqk), BF),
                   jax.ShapeDtypeStruct((b, s, nv), BF), jax.ShapeDtypeStruct((b, s, nv), BF),
                   jax.ShapeDtypeStruct((b, s, LANE), F32),
                   jax.ShapeDtypeStruct((b, s // CHUNK, 2 * C_HEADS, CHUNK), F32)],
        compiler_params=_params("parallel", "parallel"),
        name="odd_pre",
    )(x3, g_pre.reshape(1, d), wm, ws, wst,
      jnp.pad(bias, (0, LANE - 2 * C_HEADS)).reshape(1, LANE), bias.reshape(2 * C_HEADS, 1))


def _mlstm_local_body(q_ref, k_ref, colp_ref, rowp_ref,
                      pk_ref, qs_ref, en_ref, se_ref, m_ref, *, nck):
    L = CHUNK
    r = lax.broadcasted_iota(jnp.int32, (L, L), 0)
    c = lax.broadcasted_iota(jnp.int32, (L, L), 1)
    causal = r >= c
    lane = lax.broadcasted_iota(jnp.int32, (L, LANE), 1)

    @pl.when(pl.program_id(1) == 0)
    def _():
        m_ref[...] = jnp.zeros_like(m_ref)

    heads = range(C_HEADS)
    hmask = [(lane >= (h % 2) * C_DQK) & (lane < (h % 2 + 1) * C_DQK) for h in heads]
    ms = [m_ref[h:h + 1, 0:1] for h in heads]
    for ci in range(nck):
        rows = slice(ci * L, (ci + 1) * L)
        colp = colp_ref[rows, :]
        rowp = rowp_ref[ci]
        ps = [slice((h // 2) * LANE, (h // 2 + 1) * LANE) for h in heads]
        qk = [_dot_nt(jnp.where(hmask[h], q_ref[rows, ps[h]], jnp.zeros((L, LANE), BF)), k_ref[rows, ps[h]])
              for h in heads]
        lic = [colp[:, h:h + 1] for h in heads]
        bc = [colp[:, C_HEADS + h:C_HEADS + h + 1] for h in heads]
        bl = [bc[h][L - 1:L, :] for h in heads]
        dlog = [jnp.where(causal, bc[h] - rowp[C_HEADS + h:C_HEADS + h + 1, :] + rowp[h:h + 1, :], -jnp.inf)
                for h in heads]
        dmax = [jnp.max(dlog[h], axis=-1, keepdims=True) for h in heads]
        wmax = [jnp.max(bl[h] - rowp[C_HEADS + h:C_HEADS + h + 1, :] + rowp[h:h + 1, :], axis=-1, keepdims=True)
                for h in heads]
        m_old = list(ms)
        ms = [jnp.maximum(bl[h] + m_old[h], wmax[h]) for h in heads]
        inter = [bc[h] + m_old[h] for h in heads]
        mt = [jnp.maximum(inter[h], dmax[h]) for h in heads]
        en_tile = jnp.zeros((L, LANE), F32)
        for h in heads:
            pk_ref[ci, h, 0:L, :] = (jnp.exp(dlog[h] - mt[h]) * qk[h]).astype(BF)
            en_tile = jnp.where(lane == h, jnp.exp(-mt[h]), en_tile)
            se_ref[ci, h:h + 1, :] = jnp.broadcast_to(jnp.exp(bl[h] + m_old[h] - ms[h]), (1, LANE))
        en_ref[rows, :] = en_tile
        for p in range(C_HEADS // 2):
            h0, h1 = 2 * p, 2 * p + 1
            qp = q_ref[rows, ps[h0]].astype(F32)
            kpf = k_ref[rows, ps[h0]].astype(F32)
            qs = (qp * jnp.where(hmask[h0], jnp.exp(inter[h0] - mt[h0]),
                                 jnp.exp(inter[h1] - mt[h1]))).astype(BF)
            zero = jnp.zeros_like(qs)
            qs_ref[ci, p, 0:L, :] = jnp.where(hmask[h0], qs, zero)
            qs_ref[ci, p, L:2 * L, :] = jnp.where(hmask[h1], qs, zero)
            kw = kpf * jnp.where(hmask[h0], jnp.exp(bl[h0] - bc[h0] + lic[h0] - ms[h0]),
                                 jnp.exp(bl[h1] - bc[h1] + lic[h1] - ms[h1]))
            kwt = jnp.concatenate([kw, jnp.zeros_like(kw)], axis=0).T[:, :L].astype(BF)
            pk_ref[ci, h0, L:2 * L, :] = kwt[0:C_DQK]
            pk_ref[ci, h1, L:2 * L, :] = kwt[C_DQK:2 * C_DQK]
    for h in heads:
        m_ref[h:h + 1, :] = jnp.broadcast_to(ms[h], (1, LANE))


def _mlstm_local(q, k, colp, rowp, *, ts=512):
    b, s, nqk = q.shape
    nck = ts // CHUNK
    npair = C_HEADS // 2
    tok = lambda n: pl.BlockSpec((None, ts, n), lambda bi, i: (bi, i, 0))
    return pl.pallas_call(
        functools.partial(_mlstm_local_body, nck=nck),
        grid=(b, s // ts),
        in_specs=[tok(nqk), tok(nqk), tok(LANE),
                  pl.BlockSpec((None, nck, 2 * C_HEADS, CHUNK), lambda bi, i: (bi, i, 0, 0))],
        out_specs=[pl.BlockSpec((None, nck, C_HEADS, 2 * CHUNK, CHUNK), lambda bi, i: (bi, i, 0, 0, 0)),
                   pl.BlockSpec((None, nck, npair, 2 * CHUNK, LANE), lambda bi, i: (bi, i, 0, 0, 0)),
                   tok(LANE),
                   pl.BlockSpec((None, nck, C_HEADS, LANE), lambda bi, i: (bi, i, 0, 0))],
        out_shape=[jax.ShapeDtypeStruct((b, s // CHUNK, C_HEADS, 2 * CHUNK, CHUNK), BF),
                   jax.ShapeDtypeStruct((b, s // CHUNK, npair, 2 * CHUNK, LANE), BF),
                   jax.ShapeDtypeStruct((b, s, LANE), F32),
                   jax.ShapeDtypeStruct((b, s // CHUNK, C_HEADS, LANE), F32)],
        scratch_shapes=[pltpu.VMEM((C_HEADS, LANE), F32)],
        compiler_params=_params("parallel", "arbitrary"),
        name="mlstm_local",
    )(q, k, colp, rowp)


def _mlstm_scan_body(pk_ref, qs_ref, en_ref, se_ref, v_ref, og_ref, ong_ref,
                     y_ref, c_ref, *, nck, bb):
    L = CHUNK
    ones = jnp.ones((L, LANE), BF)

    @pl.when(pl.program_id(1) == 0)
    def _():
        c_ref[...] = jnp.zeros_like(c_ref)

    ch = [(bi, h) for bi in range(bb) for h in range(C_HEADS)]
    pairs = [(bi, p) for bi in range(bb) for p in range(C_HEADS // 2)]
    hsl = lambda h: slice(h * LANE, (h + 1) * LANE)
    srl = lambda h: slice((h % 2) * C_DQK, (h % 2 + 1) * C_DQK)
    for ci in range(nck):
        rows = slice(ci * L, (ci + 1) * L)
        qc = {(bi, p): _dot(qs_ref[bi, ci, p], c_ref[bi, p].astype(BF)) for bi, p in pairs}
        pv = [_dot(pk_ref[bi, ci, h], jnp.concatenate([v_ref[bi, rows, hsl(h)], ones], axis=1))
              for bi, h in ch]
        for i, (bi, h) in enumerate(ch):
            se = se_ref[bi, ci, h:h + 1, :]
            c_ref[bi, h // 2, srl(h), :] = (c_ref[bi, h // 2, srl(h), :] * jnp.concatenate([se, se], axis=1)
                                            + pv[i][L:2 * L])
        res = [qc[(bi, h // 2)][srl(h)] + pv[i][0:L] for i, (bi, h) in enumerate(ch)]
        den = [jnp.maximum(jnp.abs(res[i][:, LANE:]), en_ref[bi, rows, h:h + 1]) for i, (bi, h) in enumerate(ch)]
        hh = [_sigmoid(og_ref[bi, rows, hsl(h)].astype(F32)) * (res[i][:, :LANE] / den[i])
              for i, (bi, h) in enumerate(ch)]
        for i, (bi, h) in enumerate(ch):
            y_ref[bi, rows, hsl(h)] = _rms(hh[i], ong_ref[:, hsl(h)]).astype(BF)


def _mlstm_scan(pk, qs, en, se, v, og, onorm_g, *, ts=256, bb=4):
    b, s, nv = v.shape
    bb = min(bb, b)
    nck = ts // CHUNK
    npair = C_HEADS // 2
    tok = lambda n: pl.BlockSpec((bb, ts, n), lambda bi, i: (bi, i, 0))
    return pl.pallas_call(
        functools.partial(_mlstm_scan_body, nck=nck, bb=bb),
        grid=(b // bb, s // ts),
        in_specs=[pl.BlockSpec((bb, nck, C_HEADS, 2 * CHUNK, CHUNK), lambda bi, i: (bi, i, 0, 0, 0)),
                  pl.BlockSpec((bb, nck, npair, 2 * CHUNK, LANE), lambda bi, i: (bi, i, 0, 0, 0)),
                  tok(LANE),
                  pl.BlockSpec((bb, nck, C_HEADS, LANE), lambda bi, i: (bi, i, 0, 0)),
                  tok(nv), tok(nv), pl.BlockSpec((1, nv), lambda bi, i: (0, 0))],
        out_specs=tok(nv),
        out_shape=jax.ShapeDtypeStruct((b, s, nv), BF),
        scratch_shapes=[pltpu.VMEM((bb, npair, LANE, 2 * LANE), F32)],
        compiler_params=_params("parallel", "arbitrary"),
        name="mlstm_scan",
    )(pk, qs, en, se, v, og, onorm_g.reshape(1, nv))


def kernel(x, norm_g, ffn1_gu, ffn1_down, ffn2_gu, ffn2_down, ev_w_in, ev_w_out, ev_a_ln_g, ev_a_ws, ev_a_bs, ev_b_conv, ev_b_a_log, ev_b_dt_bias, ev_b_onorm_g, od_w_in, od_w_out, od_b_i, od_b_f, od_onorm_g):
    b, s, d = x.shape
    x2 = x.reshape(b * s, d)
    depth = norm_g.shape[0]
    for layer in range(depth):
        ng = norm_g[layer]
        j = layer // 2
        x2 = _ffn(x2, ng[0], ffn1_gu[layer], ffn1_down[layer], ng[1])
        x3 = x2.reshape(b, s, d)
        if layer % 2 == 0:
            ya, q, k, v, gate, colp, rowp = _even_pre(
                x3, ng[2], ev_w_in[j], ev_a_ln_g[j], ev_a_ws[j], ev_a_bs[j], ev_b_conv[j],
                ev_b_a_log[j], ev_b_dt_bias[j])
            u, w, qg, attn, kdt, dec = _gdn_wy(q, k, v, colp, rowp)
            yb = _gdn_scan(u, w, qg, attn, kdt, dec, gate, ev_b_onorm_g[j])
            x2 = _post(ya.reshape(b * s, -1), 0, yb.reshape(b * s, -1), 0, ev_w_out[j], ng[3], x2)
        else:
            q, k, v, og, colp, rowp = _odd_pre(x3, ng[2], od_w_in[j], od_b_i[j], od_b_f[j])
            pk, qs, en, se = _mlstm_local(q, k, colp, rowp)
            y = _mlstm_scan(pk, qs, en, se, v, og, od_onorm_g[j]).reshape(b * s, -1)
            x2 = _post(y, 0, y, 1, od_w_out[j], ng[3], x2)
        x2 = _ffn(x2, ng[4], ffn2_gu[layer], ffn2_down[layer], ng[5])
    return x2.reshape(b, s, d)
```

```python
import functools

import jax
import jax.numpy as jnp
from jax import lax
from jax.experimental import pallas as pl
from jax.experimental.pallas import tpu as pltpu

F32 = jnp.float32
BF = jnp.bfloat16
HI = lax.Precision.HIGHEST

EPS = 1e-6
LANE = 128
CHUNK = 64
A_GROUPS, A_CHUNK = 4, 128
B_HEADS, B_CONV = 4, 4
C_HEADS, C_DQK = 8, 64
GATE_CAP = 15.0
VMEM_LIMIT = 56 * 1024 * 1024


def _dot(a, b, prec=None):
    return jnp.dot(a, b, preferred_element_type=F32, precision=prec)


def _dot_nt(a, b, prec=None):
    return lax.dot_general(a, b, (((1,), (1,)), ((), ())), preferred_element_type=F32, precision=prec)


def _rms(xf, g):
    return xf * lax.rsqrt(jnp.mean(xf * xf, axis=-1, keepdims=True) + EPS) * g


def _sigmoid(x):
    return 1.0 / (1.0 + jnp.exp(-x))


def _silu(x):
    return x * _sigmoid(x)


def _gelu(x):
    return 0.5 * x * (1.0 + jnp.tanh(0.7978845608028654 * (x + 0.044715 * (x * x * x))))


def _softplus(x):
    return jnp.maximum(x, 0.0) + jnp.log1p(jnp.exp(-jnp.abs(x)))


def _chunk_tri(transpose):
    r = lax.broadcasted_iota(jnp.int32, (LANE, LANE), 1 if transpose else 0)
    c = lax.broadcasted_iota(jnp.int32, (LANE, LANE), 0 if transpose else 1)
    same = lax.shift_right_logical(r, 6) == lax.shift_right_logical(c, 6)
    return jnp.where(same, jnp.where(c <= r, 1.0, 0.0), 0.0).astype(BF)


def _split3(x):
    p1 = x.astype(BF).astype(F32)
    r1 = x - p1
    p2 = r1.astype(BF).astype(F32)
    return p1, p2, (r1 - p2).astype(BF).astype(F32)


SPLIT_LANES = 16


def _chunk_cumsum_cols(x):
    lane = lax.broadcasted_iota(jnp.int32, x.shape, 1)
    p1, p2, p3 = _split3(jnp.where(lane < SPLIT_LANES, x, 0.0))
    packed = (p1 + pltpu.roll(p2, SPLIT_LANES, axis=1) + pltpu.roll(p3, 2 * SPLIT_LANES, axis=1)).astype(BF)
    tri = _chunk_tri(False)
    res = jnp.concatenate([_dot(tri, packed[t:t + LANE]) for t in range(0, x.shape[0], LANE)], axis=0)
    return res + pltpu.roll(res, LANE - SPLIT_LANES, axis=1) + pltpu.roll(res, LANE - 2 * SPLIT_LANES, axis=1)


def _chunk_cumsum_rows(x):
    r = x.shape[0]
    p1, p2, p3 = _split3(x)
    stacked = jnp.concatenate([p1, p2, p3, jnp.zeros_like(p1)], axis=0).astype(BF)
    tri_t = _chunk_tri(True)
    res = jnp.concatenate([_dot(stacked[:, t:t + LANE], tri_t) for t in range(0, x.shape[1], LANE)], axis=1)
    return res[0:r] + res[r:2 * r] + res[2 * r:3 * r]


def _params(*sem):
    return pltpu.CompilerParams(dimension_semantics=sem, vmem_limit_bytes=VMEM_LIMIT)


def _vmem():
    return pl.BlockSpec(memory_space=pltpu.VMEM)


def _ffn_body(x_ref, gpre_ref, wgu_ref, wd_ref, gpost_ref, o_ref, acc_ref, *, tf):
    x = x_ref[...]
    xn = _rms(x, gpre_ref[...]).astype(BF)
    f = wd_ref.shape[0]

    def gate_up(j):
        return (_dot(xn, wgu_ref[:, j * tf:(j + 1) * tf]), _dot(xn, wgu_ref[:, f + j * tf:f + (j + 1) * tf]))

    n_chunks = f // tf
    gu = gate_up(0)
    for j in range(n_chunks):
        g, u = gu
        if j + 1 < n_chunks:
            gu = gate_up(j + 1)
        d = _dot((_silu(g) * u).astype(BF), wd_ref[j * tf:(j + 1) * tf, :])
        if j == 0:
            acc_ref[...] = d
        else:
            acc_ref[...] += d
    o_ref[...] = x + _rms(acc_ref[...], 0.5 * gpost_ref[...])


def _ffn(x2, g_pre, w_gu, w_down, g_post, *, tm=512, tf=256):
    m, d = x2.shape
    row = pl.BlockSpec((tm, d), lambda i: (i, 0))
    vec = pl.BlockSpec((1, d), lambda i: (0, 0))
    return pl.pallas_call(
        functools.partial(_ffn_body, tf=tf),
        grid=(m // tm,),
        in_specs=[row, vec, _vmem(), _vmem(), vec],
        out_specs=row,
        out_shape=jax.ShapeDtypeStruct((m, d), F32),
        scratch_shapes=[pltpu.VMEM((tm, d), F32)],
        compiler_params=_params("parallel"),
        name="ffn",
    )(x2, g_pre.reshape(1, d), w_gu.astype(BF), w_down.astype(BF), g_post.reshape(1, d))


def _post_body(y1_ref, y2_ref, w1_ref, w2_ref, g_ref, x_ref, o_ref):
    y = _dot(y1_ref[...], w1_ref[...]) + _dot(y2_ref[...], w2_ref[...])
    o_ref[...] = x_ref[...] + _rms(y, g_ref[...])


def _post(y1, c1, y2, c2, w_out, g_post, x2, *, tm=1024):
    m, d = x2.shape
    half = w_out.shape[0] // 2
    w = w_out.astype(BF)
    row = pl.BlockSpec((tm, d), lambda i: (i, 0))
    return pl.pallas_call(
        _post_body,
        grid=(m // tm,),
        in_specs=[pl.BlockSpec((tm, half), lambda i: (i, c1)), pl.BlockSpec((tm, half), lambda i: (i, c2)),
                  _vmem(), _vmem(), pl.BlockSpec((1, d), lambda i: (0, 0)), row],
        out_specs=row,
        out_shape=jax.ShapeDtypeStruct((m, d), F32),
        compiler_params=_params("parallel"),
        name="mixer_out",
    )(y1, y2, w[:half], w[half:], g_post.reshape(1, d), x2)


def _even_pre_body(x_ref, gpre_ref, wm_ref, ws_ref, wst_ref, lng_ref, aws_ref, bst_ref, cw_ref,
                   alog_l_ref, dtb_l_ref, alog_c_ref, dtb_c_ref,
                   y_ref, q_ref, k_ref, v_ref, gate_ref, colp_ref, rowp_ref, zbuf_ref, *, ts):
    aw = A_GROUPS * LANE
    cq = 3 * B_HEADS * LANE
    xn = _rms(x_ref[...], gpre_ref[...]).astype(BF)

    zs = _dot(xn, ws_ref[...])
    zr = _dot_nt(wst_ref[...], xn)

    @pl.when(pl.program_id(1) == 0)
    def _():
        zbuf_ref[0:8, :] = jnp.zeros((8, cq), F32)

    zbuf_ref[8:8 + ts, :] = _dot(xn, wm_ref[:, 2 * aw:2 * aw + cq])
    zu = _dot(xn, wm_ref[:, 0:aw])
    zv = _dot(xn, wm_ref[:, aw:2 * aw])

    gcol = -jnp.exp(alog_l_ref[...]) * _softplus(zs + dtb_l_ref[...])
    gc = _chunk_cumsum_cols(gcol)
    lane = lax.broadcasted_iota(jnp.int32, (ts, LANE), 1)
    colp_ref[...] = jnp.where(lane < B_HEADS, _sigmoid(zs), gc)
    grow = -jnp.exp(alog_c_ref[...]) * _softplus(zr + dtb_c_ref[...])
    gcr = _chunk_cumsum_rows(grow)
    for ci in range(ts // CHUNK):
        rowp_ref[ci] = gcr[:, ci * CHUNK:(ci + 1) * CHUNK]
    gate_ref[...] = _dot(xn, wm_ref[:, 2 * aw + cq:]).astype(BF)

    u = _gelu(zu)
    v = _gelu(zv)
    r = lax.broadcasted_iota(jnp.int32, (A_CHUNK, A_CHUNK), 0)
    c = lax.broadcasted_iota(jnp.int32, (A_CHUNK, A_CHUNK), 1)
    for g in range(A_GROUPS):
        gs = slice(g * LANE, (g + 1) * LANE)
        vg = v[:, gs]
        d = vg - jnp.mean(vg, axis=-1, keepdims=True)
        var = jnp.mean(d * d, axis=-1, keepdims=True)
        vn = (d * lax.rsqrt(var + EPS) * lng_ref[:, gs]).astype(BF)
        w = jnp.where(r >= c, aws_ref[g], 0.0).astype(BF)
        bcol = bst_ref[:, g:g + 1]
        for ci in range(ts // A_CHUNK):
            rs = slice(ci * A_CHUNK, (ci + 1) * A_CHUNK)
            mixed = _dot(w, vn[rs]) + bcol
            y_ref[rs, gs] = (u[rs, gs] * mixed).astype(BF)

    for t in range(3 * B_HEADS):
        cs = slice(t * LANE, (t + 1) * LANE)
        zz = zbuf_ref[:, cs]
        acc = cw_ref[0:1, cs] * zz
        for j in range(1, B_CONV):
            acc = pltpu.roll(acc, 1, axis=0) + cw_ref[j:j + 1, cs] * zz
        a = _silu(acc[8:])
        if t < 2 * B_HEADS:
            a = a * lax.rsqrt(jnp.sum(a * a, axis=-1, keepdims=True) + EPS)
        if t < B_HEADS:
            q_ref[:, cs] = (a * (LANE ** -0.5)).astype(BF)
        elif t < 2 * B_HEADS:
            k_ref[:, slice((t - B_HEADS) * LANE, (t - B_HEADS + 1) * LANE)] = a.astype(BF)
        else:
            v_ref[:, slice((t - 2 * B_HEADS) * LANE, (t - 2 * B_HEADS + 1) * LANE)] = a.astype(BF)
    zbuf_ref[0:8, :] = zbuf_ref[ts:ts + 8, :]


def _even_pre(x3, g_pre, w_in, a_ln_g, a_ws, a_bs, b_conv, a_log, dt_bias, *, ts=512):
    b, s, d = x3.shape
    aw = A_GROUPS * LANE
    nq = 3 * B_HEADS * LANE
    o_beta = 2 * aw + nq
    wm = jnp.concatenate([w_in[:, :o_beta], w_in[:, o_beta + 2 * B_HEADS:]], axis=1).astype(BF)
    ws = jnp.pad(w_in[:, o_beta:o_beta + 2 * B_HEADS], ((0, 0), (0, LANE - 2 * B_HEADS))).astype(BF)
    wst = jnp.pad(w_in[:, o_beta + B_HEADS:o_beta + 2 * B_HEADS].T, ((0, 8 - B_HEADS), (0, 0))).astype(BF)
    lane_pad = lambda p: jnp.pad(p, (B_HEADS, LANE - 2 * B_HEADS)).reshape(1, LANE)
    sub_pad = lambda p: jnp.pad(p, (0, 8 - B_HEADS)).reshape(8, 1)
    grid = (b, s // ts)
    tok = lambda w: pl.BlockSpec((None, ts, w), lambda bi, i: (bi, i, 0))
    vec = lambda w: pl.BlockSpec((1, w), lambda bi, i: (0, 0))
    nck = ts // CHUNK
    outs = pl.pallas_call(
        functools.partial(_even_pre_body, ts=ts),
        grid=grid,
        in_specs=[tok(d), vec(d), _vmem(), _vmem(), _vmem(), vec(aw), _vmem(), _vmem(), _vmem(),
                  vec(LANE), vec(LANE), _vmem(), _vmem()],
        out_specs=[tok(aw), tok(aw), tok(aw), tok(aw), tok(aw), tok(LANE),
                   pl.BlockSpec((None, nck, 8, CHUNK), lambda bi, i: (bi, i, 0, 0))],
        out_shape=[jax.ShapeDtypeStruct((b, s, aw), BF)] * 5
        + [jax.ShapeDtypeStruct((b, s, LANE), F32), jax.ShapeDtypeStruct((b, s // CHUNK, 8, CHUNK), F32)],
        scratch_shapes=[pltpu.VMEM((ts + 8, nq), F32)],
        compiler_params=_params("parallel", "arbitrary"),
        name="even_pre",
    )(x3, g_pre.reshape(1, d), wm, ws, wst, a_ln_g.reshape(1, aw), a_ws, a_bs.T, b_conv,
      lane_pad(a_log), lane_pad(dt_bias), sub_pad(a_log), sub_pad(dt_bias))
    return outs


def _gdn_wy_body(q_ref, k_ref, v_ref, colp_ref, rowp_ref,
                 u_ref, w_ref, qg_ref, attn_ref, kdt_ref, dec_ref, *, nck, cg):
    L = CHUNK
    r = lax.broadcasted_iota(jnp.int32, (L, L), 0)
    c = lax.broadcasted_iota(jnp.int32, (L, L), 1)
    causal = r >= c
    strict = r > c
    eye = jnp.where(r == c, 1.0, 0.0).astype(F32)
    shr = lax.shift_right_logical
    diag8 = jnp.where(shr(r, 3) == shr(c, 3), 1.0, 0.0).astype(F32)
    off = [jnp.where((shr(r, t + 1) == shr(c, t + 1)) & (shr(r, t) > shr(c, t)), 1.0, 0.0).astype(F32)
           for t in (3, 4, 5)]

    def mm(a, b):
        return _dot(a.astype(BF), b.astype(BF))

    def group(units):
        n = range(len(units))
        rows = [slice(ci * L, (ci + 1) * L) for ci, _ in units]
        hs = [slice(h * LANE, (h + 1) * LANE) for _, h in units]
        kh = [k_ref[rows[i], hs[i]] for i in n]
        khf = [kh[i].astype(F32) for i in n]
        beta = [colp_ref[rows[i], h:h + 1] for i, (_, h) in enumerate(units)]
        gc = [colp_ref[rows[i], B_HEADS + h:B_HEADS + h + 1] for i, (_, h) in enumerate(units)]
        decay = [jnp.exp(jnp.where(causal, gc[i] - rowp_ref[ci, h:h + 1, :], -jnp.inf))
                 for i, (ci, h) in enumerate(units)]
        kb = [khf[i] * beta[i] for i in n]
        kk = [_dot_nt(kb[i].astype(BF), kh[i]) for i in n]
        qk = [_dot_nt(q_ref[rows[i], hs[i]], kh[i]) for i in n]
        low = [jnp.where(strict, kk[i] * decay[i], 0.0) for i in n]
        for i, (_, h) in enumerate(units):
            attn_ref[h, rows[i], :] = jnp.where(causal, qk[i] * decay[i], 0.0).astype(BF)
        pw = [-(low[i] * diag8) for i in n]
        inv = [eye + pw[i] for i in n]
        for _ in range(2):
            pw = [mm(pw[i], pw[i]) for i in n]
            t = [mm(inv[i], pw[i]) for i in n]
            inv = [inv[i] + t[i] for i in n]
        for m in off:
            t = [mm(low[i] * m, inv[i]) for i in n]
            t = [mm(inv[i], t[i]) for i in n]
            inv = [inv[i] - t[i] for i in n]
        eg = [jnp.exp(gc[i]) for i in n]
        sol = [mm(inv[i], jnp.concatenate([v_ref[rows[i], hs[i]].astype(F32) * beta[i], kb[i] * eg[i]], axis=1))
               for i in n]
        for i, (ci, h) in enumerate(units):
            u_ref[rows[i], hs[i]] = sol[i][:, :LANE]
            w_ref[rows[i], hs[i]] = sol[i][:, LANE:].astype(BF)
            qg_ref[rows[i], hs[i]] = (q_ref[rows[i], hs[i]].astype(F32) * eg[i]).astype(BF)
            gl = gc[i][L - 1:L, :]
            kd = khf[i] * jnp.exp(gl - gc[i])
            kdt = jnp.concatenate([kd, jnp.zeros_like(kd)], axis=0).T
            kdt_ref[ci, h] = kdt[:, :L].astype(BF)
            dec_ref[ci, h:h + 1, :] = jnp.broadcast_to(jnp.exp(gl), (1, LANE))

    for c0 in range(0, nck, cg):
        group([(ci, h) for ci in range(c0, c0 + cg) for h in range(B_HEADS)])


def _gdn_wy(q, k, v, colp, rowp, *, ts=512, cg=4):
    b, s, w = q.shape
    nck = ts // CHUNK
    grid = (b, s // ts)
    tok = lambda n: pl.BlockSpec((None, ts, n), lambda bi, i: (bi, i, 0))
    return pl.pallas_call(
        functools.partial(_gdn_wy_body, nck=nck, cg=cg),
        grid=grid,
        in_specs=[tok(w), tok(w), tok(w), tok(LANE),
                  pl.BlockSpec((None, nck, 8, CHUNK), lambda bi, i: (bi, i, 0, 0))],
        out_specs=[tok(w), tok(w), tok(w),
                   pl.BlockSpec((None, B_HEADS, ts, CHUNK), lambda bi, i: (bi, 0, i, 0)),
                   pl.BlockSpec((None, nck, B_HEADS, LANE, CHUNK), lambda bi, i: (bi, i, 0, 0, 0)),
                   pl.BlockSpec((None, nck, B_HEADS, LANE), lambda bi, i: (bi, i, 0, 0))],
        out_shape=[jax.ShapeDtypeStruct((b, s, w), F32), jax.ShapeDtypeStruct((b, s, w), BF),
                   jax.ShapeDtypeStruct((b, s, w), BF),
                   jax.ShapeDtypeStruct((b, B_HEADS, s, CHUNK), BF),
                   jax.ShapeDtypeStruct((b, s // CHUNK, B_HEADS, LANE, CHUNK), BF),
                   jax.ShapeDtypeStruct((b, s // CHUNK, B_HEADS, LANE), F32)],
        compiler_params=_params("parallel", "parallel"),
        name="gdn_wy",
    )(q, k, v, colp, rowp)


def _gdn_scan_body(u_ref, w_ref, qg_ref, attn_ref, kdt_ref, dec_ref, gate_ref, ong_ref,
                   y_ref, s_ref, *, nck, bb):
    L = CHUNK

    @pl.when(pl.program_id(1) == 0)
    def _():
        s_ref[...] = jnp.zeros_like(s_ref)

    ch = [(bi, h, slice(h * LANE, (h + 1) * LANE)) for bi in range(bb) for h in range(B_HEADS)]
    for ci in range(nck):
        rows = slice(ci * L, (ci + 1) * L)
        sb = [s_ref[bi, h].astype(BF) for bi, h, _ in ch]
        ws = [_dot(w_ref[bi, rows, hs], sb[i]) for i, (bi, _, hs) in enumerate(ch)]
        qs = [_dot(qg_ref[bi, rows, hs], sb[i]) for i, (bi, _, hs) in enumerate(ch)]
        vb = [(u_ref[bi, rows, hs] - ws[i]).astype(BF) for i, (bi, _, hs) in enumerate(ch)]
        upd = [_dot(kdt_ref[bi, ci, h], vb[i]) for i, (bi, h, _) in enumerate(ch)]
        av = [_dot(attn_ref[bi, h, rows, :], vb[i]) for i, (bi, h, _) in enumerate(ch)]
        for i, (bi, h, hs) in enumerate(ch):
            s_ref[bi, h] = s_ref[bi, h] * dec_ref[bi, ci, h:h + 1, :] + upd[i]
        for i, (bi, h, hs) in enumerate(ch):
            on = _rms(qs[i] + av[i], ong_ref[...])
            y_ref[bi, rows, hs] = (on * _silu(gate_ref[bi, rows, hs].astype(F32))).astype(BF)


def _gdn_scan(u, w, qg, attn, kdt, dec, gate, onorm_g, *, ts=256, bb=8):
    b, s, wd = u.shape
    bb = min(bb, b)
    nck = ts // CHUNK
    tok = lambda n: pl.BlockSpec((bb, ts, n), lambda bi, i: (bi, i, 0))
    return pl.pallas_call(
        functools.partial(_gdn_scan_body, nck=nck, bb=bb),
        grid=(b // bb, s // ts),
        in_specs=[tok(wd), tok(wd), tok(wd),
                  pl.BlockSpec((bb, B_HEADS, ts, CHUNK), lambda bi, i: (bi, 0, i, 0)),
                  pl.BlockSpec((bb, nck, B_HEADS, LANE, CHUNK), lambda bi, i: (bi, i, 0, 0, 0)),
                  pl.BlockSpec((bb, nck, B_HEADS, LANE), lambda bi, i: (bi, i, 0, 0)),
                  tok(wd), pl.BlockSpec((1, LANE), lambda bi, i: (0, 0))],
        out_specs=tok(wd),
        out_shape=jax.ShapeDtypeStruct((b, s, wd), BF),
        scratch_shapes=[pltpu.VMEM((bb, B_HEADS, LANE, LANE), F32)],
        compiler_params=_params("parallel", "arbitrary"),
        name="gdn_scan",
    )(u, w, qg, attn, kdt, dec, gate, onorm_g.reshape(1, LANE))


def _odd_pre_body(x_ref, gpre_ref, wm_ref, ws_ref, wst_ref, bl_ref, bc_ref,
                  q_ref, k_ref, v_ref, og_ref, colp_ref, rowp_ref, *, ts):
    nqk = C_HEADS * C_DQK
    nv = C_HEADS * LANE
    xn = _rms(x_ref[...], gpre_ref[...]).astype(BF)
    zs = _dot(xn, ws_ref[...])
    zr = _dot_nt(wst_ref[...], xn)
    q_ref[...] = _dot(xn, wm_ref[:, 0:nqk]).astype(BF)
    k_ref[...] = (_dot(xn, wm_ref[:, nqk:2 * nqk]) * (C_DQK ** -0.5)).astype(BF)
    v_ref[...] = _dot(xn, wm_ref[:, 2 * nqk:2 * nqk + nv]).astype(BF)

    cap = GATE_CAP * jnp.tanh((zs + bl_ref[...]) * (1.0 / GATE_CAP))
    bcum = _chunk_cumsum_cols(-_softplus(-cap))
    lane = lax.broadcasted_iota(jnp.int32, (ts, LANE), 1)
    colp_ref[...] = jnp.where(lane < C_HEADS, cap, bcum)
    capr = GATE_CAP * jnp.tanh((zr + bc_ref[...]) * (1.0 / GATE_CAP))
    bcr = _chunk_cumsum_rows(-_softplus(-capr))
    og_ref[...] = _dot(xn, wm_ref[:, 2 * nqk + nv:]).astype(BF)
    brow = capr[0:C_HEADS] - bcr[C_HEADS:2 * C_HEADS]
    right = pltpu.roll(brow, CHUNK, axis=1)
    left = pltpu.roll(brow, ts - CHUNK, axis=1)
    npair = C_HEADS // 2
    lo = lax.broadcasted_iota(jnp.int32, (npair, LANE), 1) < CHUNK
    for ci in range(ts // CHUNK):
        tile = slice((ci // 2) * LANE, (ci // 2 + 1) * LANE)
        if ci % 2 == 0:
            rowp_ref[ci] = jnp.where(lo, brow[0:npair, tile], right[npair:C_HEADS, tile])
        else:
            rowp_ref[ci] = jnp.where(lo, left[0:npair, tile], brow[npair:C_HEADS, tile])


def _odd_pre(x3, g_pre, w_in, b_i, b_f, *, ts=512):
    b, s, d = x3.shape
    nqk = C_HEADS * C_DQK
    nv = C_HEADS * LANE
    o_g = 2 * nqk + nv
    wm = jnp.concatenate([w_in[:, :o_g], w_in[:, o_g + 2 * C_HEADS:]], axis=1).astype(BF)
    wsm = w_in[:, o_g:o_g + 2 * C_HEADS]
    ws = jnp.pad(wsm, ((0, 0), (0, LANE - 2 * C_HEADS))).astype(BF)
    bias = jnp.concatenate([b_i, b_f])
    order = jnp.array([g * C_HEADS + h for g in range(2) for h in (*range(0, C_HEADS, 2), *range(1, C_HEADS, 2))])
    wst = wsm.T[order].astype(BF)
    tok = lambda n: pl.BlockSpec((None, ts, n), lambda bi, i: (bi, i, 0))
    vec = lambda n: pl.BlockSpec((1, n), lambda bi, i: (0, 0))
    nck = ts // CHUNK
    return pl.pallas_call(
        functools.partial(_odd_pre_body, ts=ts),
        grid=(b, s // ts),
        in_specs=[tok(d), vec(d), _vmem(), _vmem(), _vmem(), vec(LANE), _vmem()],
        out_specs=[tok(nqk), tok(nqk), tok(nv), tok(nv), tok(LANE),
                   pl.BlockSpec((None, nck, C_HEADS // 2, LANE), lambda bi, i: (bi, i, 0, 0))],
        out_shape=[jax.ShapeDtypeStruct((b, s, nqk), BF), jax.ShapeDtypeStruct((b, s, nqk), BF),
                   jax.ShapeDtypeStruct((b, s, nv), BF), jax.ShapeDtypeStruct((b, s, nv), BF),
                   jax.ShapeDtypeStruct((b, s, LANE), F32),
                   jax.ShapeDtypeStruct((b, s // CHUNK, C_HEADS // 2, LANE), F32)],
        compiler_params=_params("parallel", "parallel"),
        name="odd_pre",
    )(x3, g_pre.reshape(1, d), wm, ws, wst,
      jnp.pad(bias, (0, LANE - 2 * C_HEADS)).reshape(1, LANE), bias[order].reshape(2 * C_HEADS, 1))


def _mlstm_local_body(q_ref, k_ref, colp_ref, rowp_ref,
                      pk_ref, qs_ref, en_ref, se_ref, m_ref, *, nck):
    L = CHUNK
    npair = C_HEADS // 2
    row = lax.broadcasted_iota(jnp.int32, (L, LANE), 0)
    lane = lax.broadcasted_iota(jnp.int32, (L, LANE), 1)
    lo = lane < C_DQK
    causal = row >= (lane & (L - 1))
    spread = [jnp.where(lo, 2 * p, 2 * p + 1) for p in range(npair)]
    zero = jnp.zeros((L, LANE), BF)

    @pl.when(pl.program_id(1) == 0)
    def _():
        m_ref[...] = jnp.zeros_like(m_ref)

    m = m_ref[0:1, :]
    g, m_in, wsc = [], [], []
    for ci in range(nck):
        rows = slice(ci * L, (ci + 1) * L)
        colp = colp_ref[rows, :]
        li = jnp.where(lane < C_HEADS, colp, 0.0)
        bc = jnp.where(lane < C_HEADS, pltpu.roll(colp, LANE - C_HEADS, axis=1), 0.0)
        cm = li - bc
        for s in (1, 2, 4, 8, 16, 32):
            cm = jnp.maximum(cm, jnp.where(row >= s, pltpu.roll(cm, s, axis=0), -jnp.inf))
        gi = jnp.maximum(m, cm)
        en_ref[rows, :] = jnp.exp(-bc - gi)
        bl = bc[L - 1:L, :]
        we = bl - bc + li
        m_new = jnp.maximum(bl + m, jnp.max(we, axis=0, keepdims=True))
        sev = jnp.exp(bl + m - m_new)
        for h in range(C_HEADS):
            se_ref[ci, h:h + 1, :] = jnp.broadcast_to(sev[:, h:h + 1], (1, LANE))
        g.append(gi)
        m_in.append(jnp.broadcast_to(m, (8, LANE)))
        wsc.append(we - m_new)
        m = m_new
    m_ref[...] = jnp.broadcast_to(m, m_ref.shape)

    units = [(ci, p) for ci in range(nck) for p in range(npair)]
    gp = [jnp.take_along_axis(g[ci], spread[p], axis=1) for ci, p in units]
    mp = [jnp.take_along_axis(m_in[ci], spread[p][0:8], axis=1)[0:1] for ci, p in units]
    wp = [jnp.take_along_axis(wsc[ci], spread[p], axis=1) for ci, p in units]

    qk = []
    for ci, p in units:
        rows, ps = slice(ci * L, (ci + 1) * L), slice(p * LANE, (p + 1) * LANE)
        kp = k_ref[rows, ps]
        qk.append(_dot_nt(q_ref[rows, ps], jnp.concatenate([jnp.where(lo, kp, zero), jnp.where(lo, zero, kp)], axis=0)))
    for i, (ci, p) in enumerate(units):
        rows, ps = slice(ci * L, (ci + 1) * L), slice(p * LANE, (p + 1) * LANE)
        pm = (jnp.exp(jnp.where(causal, rowp_ref[ci, p:p + 1, :] - gp[i], -jnp.inf)) * qk[i]).astype(BF)
        pk_ref[ci, p, 0:L, :] = jnp.where(lo, pm, zero)
        pk_ref[ci, p, L:2 * L, :] = jnp.where(lo, zero, pm)
        qs = (q_ref[rows, ps].astype(F32) * jnp.exp(mp[i] - gp[i])).astype(BF)
        qs_ref[ci, p, 0:L, :] = jnp.where(lo, qs, zero)
        qs_ref[ci, p, L:2 * L, :] = jnp.where(lo, zero, qs)
    for i, (ci, p) in enumerate(units):
        rows, ps = slice(ci * L, (ci + 1) * L), slice(p * LANE, (p + 1) * LANE)
        kw = k_ref[rows, ps].astype(F32) * jnp.exp(wp[i])
        kw2 = jnp.concatenate([jnp.where(lo, kw, 0.0), jnp.where(lo, 0.0, kw)], axis=0)
        pk_ref[ci, p, 2 * L:4 * L, :] = kw2.T.astype(BF)


def _mlstm_local(q, k, colp, rowp, *, ts=512):
    b, s, nqk = q.shape
    nck = ts // CHUNK
    npair = C_HEADS // 2
    tok = lambda n: pl.BlockSpec((None, ts, n), lambda bi, i: (bi, i, 0))
    return pl.pallas_call(
        functools.partial(_mlstm_local_body, nck=nck),
        grid=(b, s // ts),
        in_specs=[tok(nqk), tok(nqk), tok(LANE),
                  pl.BlockSpec((None, nck, npair, LANE), lambda bi, i: (bi, i, 0, 0))],
        out_specs=[pl.BlockSpec((None, nck, npair, 4 * CHUNK, LANE), lambda bi, i: (bi, i, 0, 0, 0)),
                   pl.BlockSpec((None, nck, npair, 2 * CHUNK, LANE), lambda bi, i: (bi, i, 0, 0, 0)),
                   tok(LANE),
                   pl.BlockSpec((None, nck, C_HEADS, LANE), lambda bi, i: (bi, i, 0, 0))],
        out_shape=[jax.ShapeDtypeStruct((b, s // CHUNK, npair, 4 * CHUNK, LANE), BF),
                   jax.ShapeDtypeStruct((b, s // CHUNK, npair, 2 * CHUNK, LANE), BF),
                   jax.ShapeDtypeStruct((b, s, LANE), F32),
                   jax.ShapeDtypeStruct((b, s // CHUNK, C_HEADS, LANE), F32)],
        scratch_shapes=[pltpu.VMEM((C_HEADS, LANE), F32)],
        compiler_params=_params("parallel", "arbitrary"),
        name="mlstm_local",
    )(q, k, colp, rowp)


def _mlstm_scan_body(pk_ref, qs_ref, en_ref, se_ref, v_ref, og_ref, ong_ref,
                     y_ref, c_ref, *, nck, bb):
    L = CHUNK
    ones = jnp.ones((L, LANE), BF)

    @pl.when(pl.program_id(1) == 0)
    def _():
        c_ref[...] = jnp.zeros_like(c_ref)

    ch = [(bi, h) for bi in range(bb) for h in range(C_HEADS)]
    pairs = [(bi, p) for bi in range(bb) for p in range(C_HEADS // 2)]
    hsl = lambda h: slice(h * LANE, (h + 1) * LANE)
    srl = lambda h: slice((h % 2) * C_DQK, (h % 2 + 1) * C_DQK)
    for ci in range(nck):
        rows = slice(ci * L, (ci + 1) * L)
        qc = {(bi, p): _dot(qs_ref[bi, ci, p], c_ref[bi, p].astype(BF)) for bi, p in pairs}
        pv = {}
        for bi, p in pairs:
            vst = jnp.concatenate(
                [jnp.concatenate([v_ref[bi, rows, hsl(2 * p + sub)], ones], axis=1) for sub in range(2)], axis=0)
            pv[(bi, p)] = _dot(pk_ref[bi, ci, p], vst)
        for bi, h in ch:
            se = se_ref[bi, ci, h:h + 1, :]
            c_ref[bi, h // 2, srl(h), :] = (c_ref[bi, h // 2, srl(h), :] * jnp.concatenate([se, se], axis=1)
                                            + pv[(bi, h // 2)][2 * L + (h % 2) * L:3 * L + (h % 2) * L])
        res = [qc[(bi, h // 2)][srl(h)] + pv[(bi, h // 2)][srl(h)] for bi, h in ch]
        den = [jnp.maximum(jnp.abs(res[i][:, LANE:]), en_ref[bi, rows, h:h + 1]) for i, (bi, h) in enumerate(ch)]
        hh = [_sigmoid(og_ref[bi, rows, hsl(h)].astype(F32)) * (res[i][:, :LANE] / den[i])
              for i, (bi, h) in enumerate(ch)]
        for i, (bi, h) in enumerate(ch):
            y_ref[bi, rows, hsl(h)] = _rms(hh[i], ong_ref[:, hsl(h)]).astype(BF)


def _mlstm_scan(pk, qs, en, se, v, og, onorm_g, *, ts=256, bb=4):
    b, s, nv = v.shape
    bb = min(bb, b)
    nck = ts // CHUNK
    npair = C_HEADS // 2
    tok = lambda n: pl.BlockSpec((bb, ts, n), lambda bi, i: (bi, i, 0))
    return pl.pallas_call(
        functools.partial(_mlstm_scan_body, nck=nck, bb=bb),
        grid=(b // bb, s // ts),
        in_specs=[pl.BlockSpec((bb, nck, npair, 4 * CHUNK, LANE), lambda bi, i: (bi, i, 0, 0, 0)),
                  pl.BlockSpec((bb, nck, npair, 2 * CHUNK, LANE), lambda bi, i: (bi, i, 0, 0, 0)),
                  tok(LANE),
                  pl.BlockSpec((bb, nck, C_HEADS, LANE), lambda bi, i: (bi, i, 0, 0)),
                  tok(nv), tok(nv), pl.BlockSpec((1, nv), lambda bi, i: (0, 0))],
        out_specs=tok(nv),
        out_shape=jax.ShapeDtypeStruct((b, s, nv), BF),
        scratch_shapes=[pltpu.VMEM((bb, npair, LANE, 2 * LANE), F32)],
        compiler_params=_params("parallel", "arbitrary"),
        name="mlstm_scan",
    )(pk, qs, en, se, v, og, onorm_g.reshape(1, nv))


def kernel(x, norm_g, ffn1_gu, ffn1_down, ffn2_gu, ffn2_down, ev_w_in, ev_w_out, ev_a_ln_g, ev_a_ws, ev_a_bs, ev_b_conv, ev_b_a_log, ev_b_dt_bias, ev_b_onorm_g, od_w_in, od_w_out, od_b_i, od_b_f, od_onorm_g):
    b, s, d = x.shape
    x2 = x.reshape(b * s, d)
    depth = norm_g.shape[0]
    for layer in range(depth):
        ng = norm_g[layer]
        j = layer // 2
        x2 = _ffn(x2, ng[0], ffn1_gu[layer], ffn1_down[layer], ng[1])
        x3 = x2.reshape(b, s, d)
        if layer % 2 == 0:
            ya, q, k, v, gate, colp, rowp = _even_pre(
                x3, ng[2], ev_w_in[j], ev_a_ln_g[j], ev_a_ws[j], ev_a_bs[j], ev_b_conv[j],
                ev_b_a_log[j], ev_b_dt_bias[j])
            u, w, qg, attn, kdt, dec = _gdn_wy(q, k, v, colp, rowp)
            yb = _gdn_scan(u, w, qg, attn, kdt, dec, gate, ev_b_onorm_g[j])
            x2 = _post(ya.reshape(b * s, -1), 0, yb.reshape(b * s, -1), 0, ev_w_out[j], ng[3], x2)
        else:
            q, k, v, og, colp, rowp = _odd_pre(x3, ng[2], od_w_in[j], od_b_i[j], od_b_f[j])
            pk, qs, en, se = _mlstm_local(q, k, colp, rowp)
            y = _mlstm_scan(pk, qs, en, se, v, og, od_onorm_g[j]).reshape(b * s, -1)
            x2 = _post(y, 0, y, 1, od_w_out[j], ng[3], x2)
        x2 = _ffn(x2, ng[4], ffn2_gu[layer], ffn2_down[layer], ng[5])
    return x2.reshape(b, s, d)
```

```python
import functools

import jax
import jax.numpy as jnp
from jax import lax
from jax.experimental import pallas as pl
from jax.experimental.pallas import tpu as pltpu

F32 = jnp.float32
BF = jnp.bfloat16
HI = lax.Precision.HIGHEST

EPS = 1e-6
LANE = 128
CHUNK = 64
A_GROUPS, A_CHUNK = 4, 128
B_HEADS, B_CONV = 4, 4
C_HEADS, C_DQK = 8, 64
GATE_CAP = 15.0
VMEM_LIMIT = 56 * 1024 * 1024


def _dot(a, b, prec=None):
    return jnp.dot(a, b, preferred_element_type=F32, precision=prec)


def _dot_nt(a, b, prec=None):
    return lax.dot_general(a, b, (((1,), (1,)), ((), ())), preferred_element_type=F32, precision=prec)


def _rms(xf, g):
    return xf * lax.rsqrt(jnp.mean(xf * xf, axis=-1, keepdims=True) + EPS) * g


def _sigmoid(x):
    return 1.0 / (1.0 + jnp.exp(-x))


def _silu(x):
    return x * _sigmoid(x)


def _gelu(x):
    return 0.5 * x * (1.0 + jnp.tanh(0.7978845608028654 * (x + 0.044715 * (x * x * x))))


def _softplus(x):
    return jnp.maximum(x, 0.0) + jnp.log1p(jnp.exp(-jnp.abs(x)))


def _chunk_tri(transpose):
    r = lax.broadcasted_iota(jnp.int32, (LANE, LANE), 1 if transpose else 0)
    c = lax.broadcasted_iota(jnp.int32, (LANE, LANE), 0 if transpose else 1)
    same = lax.shift_right_logical(r, 6) == lax.shift_right_logical(c, 6)
    return jnp.where(same, jnp.where(c <= r, 1.0, 0.0), 0.0).astype(BF)


def _split3(x):
    p1 = x.astype(BF).astype(F32)
    r1 = x - p1
    p2 = r1.astype(BF).astype(F32)
    return p1, p2, (r1 - p2).astype(BF).astype(F32)


SPLIT_LANES = 16


def _chunk_cumsum_cols(x):
    lane = lax.broadcasted_iota(jnp.int32, x.shape, 1)
    p1, p2, p3 = _split3(jnp.where(lane < SPLIT_LANES, x, 0.0))
    packed = (p1 + pltpu.roll(p2, SPLIT_LANES, axis=1) + pltpu.roll(p3, 2 * SPLIT_LANES, axis=1)).astype(BF)
    tri = _chunk_tri(False)
    res = jnp.concatenate([_dot(tri, packed[t:t + LANE]) for t in range(0, x.shape[0], LANE)], axis=0)
    return res + pltpu.roll(res, LANE - SPLIT_LANES, axis=1) + pltpu.roll(res, LANE - 2 * SPLIT_LANES, axis=1)


def _chunk_cumsum_rows(x):
    r = x.shape[0]
    p1, p2, p3 = _split3(x)
    stacked = jnp.concatenate([p1, p2, p3, jnp.zeros_like(p1)], axis=0).astype(BF)
    tri_t = _chunk_tri(True)
    res = jnp.concatenate([_dot(stacked[:, t:t + LANE], tri_t) for t in range(0, x.shape[1], LANE)], axis=1)
    return res[0:r] + res[r:2 * r] + res[2 * r:3 * r]


def _params(*sem):
    return pltpu.CompilerParams(dimension_semantics=sem, vmem_limit_bytes=VMEM_LIMIT)


def _vmem():
    return pl.BlockSpec(memory_space=pltpu.VMEM)


def _swiglu_step(x, gpre_ref, wgu_ref, wd_ref, gpost_ref, o_ref, acc_ref, tf):
    xn = _rms(x, gpre_ref[...]).astype(BF)
    f = wd_ref.shape[0]

    def gate_up(j):
        return (_dot(xn, wgu_ref[:, j * tf:(j + 1) * tf]), _dot(xn, wgu_ref[:, f + j * tf:f + (j + 1) * tf]))

    n_chunks = f // tf
    gu = gate_up(0)
    for j in range(n_chunks):
        g, u = gu
        if j + 1 < n_chunks:
            gu = gate_up(j + 1)
        d = _dot((_silu(g) * u).astype(BF), wd_ref[j * tf:(j + 1) * tf, :])
        if j == 0:
            acc_ref[...] = d
        else:
            acc_ref[...] += d
    o_ref[...] = x + _rms(acc_ref[...], 0.5 * gpost_ref[...])


def _ffn_body(x_ref, gpre_ref, wgu_ref, wd_ref, gpost_ref, o_ref, acc_ref, *, tf):
    _swiglu_step(x_ref[...], gpre_ref, wgu_ref, wd_ref, gpost_ref, o_ref, acc_ref, tf)


def _ffn(x2, g_pre, w_gu, w_down, g_post, *, tm=512, tf=256):
    m, d = x2.shape
    row = pl.BlockSpec((tm, d), lambda i: (i, 0))
    vec = pl.BlockSpec((1, d), lambda i: (0, 0))
    return pl.pallas_call(
        functools.partial(_ffn_body, tf=tf),
        grid=(m // tm,),
        in_specs=[row, vec, _vmem(), _vmem(), vec],
        out_specs=row,
        out_shape=jax.ShapeDtypeStruct((m, d), F32),
        scratch_shapes=[pltpu.VMEM((tm, d), F32)],
        compiler_params=_params("parallel"),
        name="ffn",
    )(x2, g_pre.reshape(1, d), w_gu.astype(BF), w_down.astype(BF), g_post.reshape(1, d))


def _post_ffn_body(y1_ref, y2_ref, w1_ref, w2_ref, gmix_ref, x_ref, gpre_ref, wgu_ref, wd_ref, gpost_ref,
                   o_ref, acc_ref, *, tf):
    y = _dot(y1_ref[...], w1_ref[...]) + _dot(y2_ref[...], w2_ref[...])
    x = x_ref[...] + _rms(y, gmix_ref[...])
    _swiglu_step(x, gpre_ref, wgu_ref, wd_ref, gpost_ref, o_ref, acc_ref, tf)


def _post_ffn(y1, c1, y2, c2, w_out, g_mix, x2, g_pre, w_gu, w_down, g_post, *, tm=512, tf=256):
    m, d = x2.shape
    half = w_out.shape[0] // 2
    w = w_out.astype(BF)
    row = pl.BlockSpec((tm, d), lambda i: (i, 0))
    vec = pl.BlockSpec((1, d), lambda i: (0, 0))
    return pl.pallas_call(
        functools.partial(_post_ffn_body, tf=tf),
        grid=(m // tm,),
        in_specs=[pl.BlockSpec((tm, half), lambda i: (i, c1)), pl.BlockSpec((tm, half), lambda i: (i, c2)),
                  _vmem(), _vmem(), vec, row, vec, _vmem(), _vmem(), vec],
        out_specs=row,
        out_shape=jax.ShapeDtypeStruct((m, d), F32),
        scratch_shapes=[pltpu.VMEM((tm, d), F32)],
        compiler_params=_params("parallel"),
        name="post_ffn",
    )(y1, y2, w[:half], w[half:], g_mix.reshape(1, d), x2, g_pre.reshape(1, d),
      w_gu.astype(BF), w_down.astype(BF), g_post.reshape(1, d))


def _even_pre_body(x_ref, gpre_ref, wm_ref, wst_ref, lng_ref, aws_ref, bst_ref, cw_ref,
                   alog_c_ref, dtb_c_ref,
                   y_ref, q_ref, k_ref, v_ref, gate_ref, colp_ref, rowp_ref, zbuf_ref, *, ts):
    aw = A_GROUPS * LANE
    cq = 3 * B_HEADS * LANE
    xn = _rms(x_ref[...], gpre_ref[...]).astype(BF)

    zr = _dot_nt(wst_ref[...], xn)

    @pl.when(pl.program_id(1) == 0)
    def _():
        zbuf_ref[0:8, :] = jnp.zeros((8, cq), F32)

    zbuf_ref[8:8 + ts, :] = _dot(xn, wm_ref[:, 2 * aw:2 * aw + cq])
    zu = _dot(xn, wm_ref[:, 0:aw])
    zv = _dot(xn, wm_ref[:, aw:2 * aw])

    grow = -jnp.exp(alog_c_ref[...]) * _softplus(zr + dtb_c_ref[...])
    sub = lax.broadcasted_iota(jnp.int32, (8, ts), 0)
    gates = jnp.where(sub < B_HEADS, _sigmoid(zr), _chunk_cumsum_rows(grow))
    for ci in range(ts // CHUNK):
        rowp_ref[ci] = gates[:, ci * CHUNK:(ci + 1) * CHUNK]
    padded = jnp.concatenate([gates, jnp.zeros((LANE - 8, ts), F32)], axis=0)
    for t in range(0, ts, LANE):
        colp_ref[t:t + LANE, :] = padded[:, t:t + LANE].T
    gate_ref[...] = _dot(xn, wm_ref[:, 2 * aw + cq:]).astype(BF)

    u = _gelu(zu)
    v = _gelu(zv)
    r = lax.broadcasted_iota(jnp.int32, (A_CHUNK, A_CHUNK), 0)
    c = lax.broadcasted_iota(jnp.int32, (A_CHUNK, A_CHUNK), 1)
    for g in range(A_GROUPS):
        gs = slice(g * LANE, (g + 1) * LANE)
        vg = v[:, gs]
        d = vg - jnp.mean(vg, axis=-1, keepdims=True)
        var = jnp.mean(d * d, axis=-1, keepdims=True)
        vn = (d * lax.rsqrt(var + EPS) * lng_ref[:, gs]).astype(BF)
        w = jnp.where(r >= c, aws_ref[g], 0.0).astype(BF)
        bcol = bst_ref[:, g:g + 1]
        for ci in range(ts // A_CHUNK):
            rs = slice(ci * A_CHUNK, (ci + 1) * A_CHUNK)
            mixed = _dot(w, vn[rs]) + bcol
            y_ref[rs, gs] = (u[rs, gs] * mixed).astype(BF)

    for t in range(3 * B_HEADS):
        cs = slice(t * LANE, (t + 1) * LANE)
        zz = zbuf_ref[:, cs]
        acc = cw_ref[0:1, cs] * zz
        for j in range(1, B_CONV):
            acc = pltpu.roll(acc, 1, axis=0) + cw_ref[j:j + 1, cs] * zz
        a = _silu(acc[8:])
        if t < 2 * B_HEADS:
            a = a * lax.rsqrt(jnp.sum(a * a, axis=-1, keepdims=True) + EPS)
        if t < B_HEADS:
            q_ref[:, cs] = (a * (LANE ** -0.5)).astype(BF)
        elif t < 2 * B_HEADS:
            k_ref[:, slice((t - B_HEADS) * LANE, (t - B_HEADS + 1) * LANE)] = a.astype(BF)
        else:
            v_ref[:, slice((t - 2 * B_HEADS) * LANE, (t - 2 * B_HEADS + 1) * LANE)] = a.astype(BF)
    zbuf_ref[0:8, :] = zbuf_ref[ts:ts + 8, :]


def _even_pre(x3, g_pre, w_in, a_ln_g, a_ws, a_bs, b_conv, a_log, dt_bias, *, ts=512):
    b, s, d = x3.shape
    aw = A_GROUPS * LANE
    nq = 3 * B_HEADS * LANE
    o_beta = 2 * aw + nq
    wm = jnp.concatenate([w_in[:, :o_beta], w_in[:, o_beta + 2 * B_HEADS:]], axis=1).astype(BF)
    wst = w_in[:, o_beta:o_beta + 2 * B_HEADS].T.astype(BF)
    sub_pad = lambda p: jnp.pad(p, (B_HEADS, 0)).reshape(2 * B_HEADS, 1)
    grid = (b, s // ts)
    tok = lambda w: pl.BlockSpec((None, ts, w), lambda bi, i: (bi, i, 0))
    vec = lambda w: pl.BlockSpec((1, w), lambda bi, i: (0, 0))
    nck = ts // CHUNK
    outs = pl.pallas_call(
        functools.partial(_even_pre_body, ts=ts),
        grid=grid,
        in_specs=[tok(d), vec(d), _vmem(), _vmem(), vec(aw), _vmem(), _vmem(), _vmem(), _vmem(), _vmem()],
        out_specs=[tok(aw), tok(aw), tok(aw), tok(aw), tok(aw), tok(LANE),
                   pl.BlockSpec((None, nck, 8, CHUNK), lambda bi, i: (bi, i, 0, 0))],
        out_shape=[jax.ShapeDtypeStruct((b, s, aw), BF)] * 5
        + [jax.ShapeDtypeStruct((b, s, LANE), F32), jax.ShapeDtypeStruct((b, s // CHUNK, 8, CHUNK), F32)],
        scratch_shapes=[pltpu.VMEM((ts + 8, nq), F32)],
        compiler_params=_params("parallel", "arbitrary"),
        name="even_pre",
    )(x3, g_pre.reshape(1, d), wm, wst, a_ln_g.reshape(1, aw), a_ws, a_bs.T, b_conv,
      sub_pad(a_log), sub_pad(dt_bias))
    return outs


def _gdn_wy_body(q_ref, k_ref, v_ref, colp_ref, rowp_ref,
                 u_ref, w_ref, qg_ref, attn_ref, kdt_ref, dec_ref, *, nck, cg):
    L = CHUNK
    r = lax.broadcasted_iota(jnp.int32, (L, L), 0)
    c = lax.broadcasted_iota(jnp.int32, (L, L), 1)
    causal = r >= c
    strict = r > c
    eye = jnp.where(r == c, 1.0, 0.0).astype(F32)
    shr = lax.shift_right_logical
    diag8 = jnp.where(shr(r, 3) == shr(c, 3), 1.0, 0.0).astype(F32)
    off = [jnp.where((shr(r, t + 1) == shr(c, t + 1)) & (shr(r, t) > shr(c, t)), 1.0, 0.0).astype(F32)
           for t in (3, 4, 5)]

    def mm(a, b):
        return _dot(a.astype(BF), b.astype(BF))

    def group(units):
        n = range(len(units))
        rows = [slice(ci * L, (ci + 1) * L) for ci, _ in units]
        hs = [slice(h * LANE, (h + 1) * LANE) for _, h in units]
        kh = [k_ref[rows[i], hs[i]] for i in n]
        khf = [kh[i].astype(F32) for i in n]
        beta = [colp_ref[rows[i], h:h + 1] for i, (_, h) in enumerate(units)]
        gc = [colp_ref[rows[i], B_HEADS + h:B_HEADS + h + 1] for i, (_, h) in enumerate(units)]
        decay = [jnp.exp(jnp.where(causal, gc[i] - rowp_ref[ci, B_HEADS + h:B_HEADS + h + 1, :], -jnp.inf))
                 for i, (ci, h) in enumerate(units)]
        kb = [khf[i] * beta[i] for i in n]
        kk = [_dot_nt(kb[i].astype(BF), kh[i]) for i in n]
        qk = [_dot_nt(q_ref[rows[i], hs[i]], kh[i]) for i in n]
        low = [jnp.where(strict, kk[i] * decay[i], 0.0) for i in n]
        for i, (_, h) in enumerate(units):
            attn_ref[h, rows[i], :] = jnp.where(causal, qk[i] * decay[i], 0.0).astype(BF)
        pw = [-(low[i] * diag8) for i in n]
        inv = [eye + pw[i] for i in n]
        for _ in range(2):
            pw = [mm(pw[i], pw[i]) for i in n]
            t = [mm(inv[i], pw[i]) for i in n]
            inv = [inv[i] + t[i] for i in n]
        for m in off:
            t = [mm(low[i] * m, inv[i]) for i in n]
            t = [mm(inv[i], t[i]) for i in n]
            inv = [inv[i] - t[i] for i in n]
        eg = [jnp.exp(gc[i]) for i in n]
        sol = [mm(inv[i], jnp.concatenate([v_ref[rows[i], hs[i]].astype(F32) * beta[i], kb[i] * eg[i]], axis=1))
               for i in n]
        for i, (ci, h) in enumerate(units):
            u_ref[rows[i], hs[i]] = sol[i][:, :LANE]
            w_ref[rows[i], hs[i]] = sol[i][:, LANE:].astype(BF)
            qg_ref[rows[i], hs[i]] = (q_ref[rows[i], hs[i]].astype(F32) * eg[i]).astype(BF)
            gl = gc[i][L - 1:L, :]
            kd = khf[i] * jnp.exp(gl - gc[i])
            kdt = jnp.concatenate([kd, jnp.zeros_like(kd)], axis=0).T
            kdt_ref[ci, h] = kdt[:, :L].astype(BF)
            dec_ref[ci, h:h + 1, :] = jnp.broadcast_to(jnp.exp(gl), (1, LANE))

    for c0 in range(0, nck, cg):
        group([(ci, h) for ci in range(c0, c0 + cg) for h in range(B_HEADS)])


def _gdn_wy(q, k, v, colp, rowp, *, ts=512, cg=4):
    b, s, w = q.shape
    nck = ts // CHUNK
    grid = (b, s // ts)
    tok = lambda n: pl.BlockSpec((None, ts, n), lambda bi, i: (bi, i, 0))
    return pl.pallas_call(
        functools.partial(_gdn_wy_body, nck=nck, cg=cg),
        grid=grid,
        in_specs=[tok(w), tok(w), tok(w), tok(LANE),
                  pl.BlockSpec((None, nck, 8, CHUNK), lambda bi, i: (bi, i, 0, 0))],
        out_specs=[tok(w), tok(w), tok(w),
                   pl.BlockSpec((None, B_HEADS, ts, CHUNK), lambda bi, i: (bi, 0, i, 0)),
                   pl.BlockSpec((None, nck, B_HEADS, LANE, CHUNK), lambda bi, i: (bi, i, 0, 0, 0)),
                   pl.BlockSpec((None, nck, B_HEADS, LANE), lambda bi, i: (bi, i, 0, 0))],
        out_shape=[jax.ShapeDtypeStruct((b, s, w), F32), jax.ShapeDtypeStruct((b, s, w), BF),
                   jax.ShapeDtypeStruct((b, s, w), BF),
                   jax.ShapeDtypeStruct((b, B_HEADS, s, CHUNK), BF),
                   jax.ShapeDtypeStruct((b, s // CHUNK, B_HEADS, LANE, CHUNK), BF),
                   jax.ShapeDtypeStruct((b, s // CHUNK, B_HEADS, LANE), F32)],
        compiler_params=_params("parallel", "parallel"),
        name="gdn_wy",
    )(q, k, v, colp, rowp)


def _gdn_scan_body(u_ref, w_ref, qg_ref, attn_ref, kdt_ref, dec_ref, gate_ref, ong_ref,
                   y_ref, s_ref, *, nck, bb):
    L = CHUNK

    @pl.when(pl.program_id(1) == 0)
    def _():
        s_ref[...] = jnp.zeros_like(s_ref)

    ch = [(bi, h, slice(h * LANE, (h + 1) * LANE)) for bi in range(bb) for h in range(B_HEADS)]
    for ci in range(nck):
        rows = slice(ci * L, (ci + 1) * L)
        sb = [s_ref[bi, h].astype(BF) for bi, h, _ in ch]
        ws = [_dot(w_ref[bi, rows, hs], sb[i]) for i, (bi, _, hs) in enumerate(ch)]
        qs = [_dot(qg_ref[bi, rows, hs], sb[i]) for i, (bi, _, hs) in enumerate(ch)]
        vb = [(u_ref[bi, rows, hs] - ws[i]).astype(BF) for i, (bi, _, hs) in enumerate(ch)]
        upd = [_dot(kdt_ref[bi, ci, h], vb[i]) for i, (bi, h, _) in enumerate(ch)]
        av = [_dot(attn_ref[bi, h, rows, :], vb[i]) for i, (bi, h, _) in enumerate(ch)]
        for i, (bi, h, hs) in enumerate(ch):
            s_ref[bi, h] = s_ref[bi, h] * dec_ref[bi, ci, h:h + 1, :] + upd[i]
        for i, (bi, h, hs) in enumerate(ch):
            on = _rms(qs[i] + av[i], ong_ref[...])
            y_ref[bi, rows, hs] = (on * _silu(gate_ref[bi, rows, hs].astype(F32))).astype(BF)


def _gdn_scan(u, w, qg, attn, kdt, dec, gate, onorm_g, *, ts=256, bb=8):
    b, s, wd = u.shape
    bb = min(bb, b)
    nck = ts // CHUNK
    tok = lambda n: pl.BlockSpec((bb, ts, n), lambda bi, i: (bi, i, 0))
    return pl.pallas_call(
        functools.partial(_gdn_scan_body, nck=nck, bb=bb),
        grid=(b // bb, s // ts),
        in_specs=[tok(wd), tok(wd), tok(wd),
                  pl.BlockSpec((bb, B_HEADS, ts, CHUNK), lambda bi, i: (bi, 0, i, 0)),
                  pl.BlockSpec((bb, nck, B_HEADS, LANE, CHUNK), lambda bi, i: (bi, i, 0, 0, 0)),
                  pl.BlockSpec((bb, nck, B_HEADS, LANE), lambda bi, i: (bi, i, 0, 0)),
                  tok(wd), pl.BlockSpec((1, LANE), lambda bi, i: (0, 0))],
        out_specs=tok(wd),
        out_shape=jax.ShapeDtypeStruct((b, s, wd), BF),
        scratch_shapes=[pltpu.VMEM((bb, B_HEADS, LANE, LANE), F32)],
        compiler_params=_params("parallel", "arbitrary"),
        name="gdn_scan",
    )(u, w, qg, attn, kdt, dec, gate, onorm_g.reshape(1, LANE))


def _head_lane(h):
    return h // 2 + (C_HEADS // 2) * (h % 2)


def _odd_pre_body(x_ref, gpre_ref, wm_ref, wst_ref, bc_ref,
                  q_ref, k_ref, v_ref, og_ref, colp_ref, rowp_ref, *, ts):
    nqk = C_HEADS * C_DQK
    nv = C_HEADS * LANE
    xn = _rms(x_ref[...], gpre_ref[...]).astype(BF)
    zr = _dot_nt(wst_ref[...], xn)
    q_ref[...] = _dot(xn, wm_ref[:, 0:nqk]).astype(BF)
    k_ref[...] = (_dot(xn, wm_ref[:, nqk:2 * nqk]) * (C_DQK ** -0.5)).astype(BF)
    v_ref[...] = _dot(xn, wm_ref[:, 2 * nqk:2 * nqk + nv]).astype(BF)

    capr = GATE_CAP * jnp.tanh((zr + bc_ref[...]) * (1.0 / GATE_CAP))
    bcr = _chunk_cumsum_rows(-_softplus(-capr))
    og_ref[...] = _dot(xn, wm_ref[:, 2 * nqk + nv:]).astype(BF)
    sub = lax.broadcasted_iota(jnp.int32, (2 * C_HEADS, ts), 0)
    padded = jnp.concatenate([jnp.where(sub < C_HEADS, capr, bcr), jnp.zeros((LANE - 2 * C_HEADS, ts), F32)], axis=0)
    for t in range(0, ts, LANE):
        colp_ref[t:t + LANE, :] = padded[:, t:t + LANE].T
    brow = capr[0:C_HEADS] - bcr[C_HEADS:2 * C_HEADS]
    right = pltpu.roll(brow, CHUNK, axis=1)
    left = pltpu.roll(brow, ts - CHUNK, axis=1)
    npair = C_HEADS // 2
    lo = lax.broadcasted_iota(jnp.int32, (npair, LANE), 1) < CHUNK
    for ci in range(ts // CHUNK):
        tile = slice((ci // 2) * LANE, (ci // 2 + 1) * LANE)
        if ci % 2 == 0:
            rowp_ref[ci] = jnp.where(lo, brow[0:npair, tile], right[npair:C_HEADS, tile])
        else:
            rowp_ref[ci] = jnp.where(lo, left[0:npair, tile], brow[npair:C_HEADS, tile])


def _odd_pre(x3, g_pre, w_in, b_i, b_f, *, ts=512):
    b, s, d = x3.shape
    nqk = C_HEADS * C_DQK
    nv = C_HEADS * LANE
    o_g = 2 * nqk + nv
    wm = jnp.concatenate([w_in[:, :o_g], w_in[:, o_g + 2 * C_HEADS:]], axis=1).astype(BF)
    wsm = w_in[:, o_g:o_g + 2 * C_HEADS]
    bias = jnp.concatenate([b_i, b_f])
    order = jnp.array([g * C_HEADS + h for g in range(2) for h in (*range(0, C_HEADS, 2), *range(1, C_HEADS, 2))])
    wst = wsm.T[order].astype(BF)
    tok = lambda n: pl.BlockSpec((None, ts, n), lambda bi, i: (bi, i, 0))
    vec = lambda n: pl.BlockSpec((1, n), lambda bi, i: (0, 0))
    nck = ts // CHUNK
    return pl.pallas_call(
        functools.partial(_odd_pre_body, ts=ts),
        grid=(b, s // ts),
        in_specs=[tok(d), vec(d), _vmem(), _vmem(), _vmem()],
        out_specs=[tok(nqk), tok(nqk), tok(nv), tok(nv), tok(LANE),
                   pl.BlockSpec((None, nck, C_HEADS // 2, LANE), lambda bi, i: (bi, i, 0, 0))],
        out_shape=[jax.ShapeDtypeStruct((b, s, nqk), BF), jax.ShapeDtypeStruct((b, s, nqk), BF),
                   jax.ShapeDtypeStruct((b, s, nv), BF), jax.ShapeDtypeStruct((b, s, nv), BF),
                   jax.ShapeDtypeStruct((b, s, LANE), F32),
                   jax.ShapeDtypeStruct((b, s // CHUNK, C_HEADS // 2, LANE), F32)],
        compiler_params=_params("parallel", "parallel"),
        name="odd_pre",
    )(x3, g_pre.reshape(1, d), wm, wst, bias[order].reshape(2 * C_HEADS, 1))


def _mlstm_local_body(q_ref, k_ref, colp_ref, rowp_ref,
                      pk_ref, qs_ref, en_ref, se_ref, m_ref, *, nck):
    L = CHUNK
    npair = C_HEADS // 2
    row = lax.broadcasted_iota(jnp.int32, (L, LANE), 0)
    lane = lax.broadcasted_iota(jnp.int32, (L, LANE), 1)
    lo = lane < C_DQK
    causal = row >= (lane & (L - 1))
    spread = [jnp.where(lo, _head_lane(2 * p), _head_lane(2 * p + 1)) for p in range(npair)]
    zero = jnp.zeros((L, LANE), BF)

    @pl.when(pl.program_id(1) == 0)
    def _():
        m_ref[...] = jnp.zeros_like(m_ref)

    m = m_ref[0:1, :]
    g, m_in, wsc = [], [], []
    for ci in range(nck):
        rows = slice(ci * L, (ci + 1) * L)
        colp = colp_ref[rows, :]
        li = jnp.where(lane < C_HEADS, colp, 0.0)
        bc = jnp.where(lane < C_HEADS, pltpu.roll(colp, LANE - C_HEADS, axis=1), 0.0)
        cm = li - bc
        for s in (1, 2, 4, 8, 16, 32):
            cm = jnp.maximum(cm, jnp.where(row >= s, pltpu.roll(cm, s, axis=0), -jnp.inf))
        gi = jnp.maximum(m, cm)
        en_ref[rows, :] = jnp.exp(-bc - gi)
        bl = bc[L - 1:L, :]
        we = bl - bc + li
        m_new = jnp.maximum(bl + m, jnp.max(we, axis=0, keepdims=True))
        sev = jnp.exp(bl + m - m_new)
        for h in range(C_HEADS):
            se_ref[ci, h:h + 1, :] = jnp.broadcast_to(sev[:, _head_lane(h):_head_lane(h) + 1], (1, LANE))
        g.append(gi)
        m_in.append(jnp.broadcast_to(m, (8, LANE)))
        wsc.append(we - m_new)
        m = m_new
    m_ref[...] = jnp.broadcast_to(m, m_ref.shape)

    units = [(ci, p) for ci in range(nck) for p in range(npair)]
    gp = [jnp.take_along_axis(g[ci], spread[p], axis=1) for ci, p in units]
    mp = [jnp.take_along_axis(m_in[ci], spread[p][0:8], axis=1)[0:1] for ci, p in units]
    wp = [jnp.take_along_axis(wsc[ci], spread[p], axis=1) for ci, p in units]

    qk = []
    for ci, p in units:
        rows, ps = slice(ci * L, (ci + 1) * L), slice(p * LANE, (p + 1) * LANE)
        kp = k_ref[rows, ps]
        qk.append(_dot_nt(q_ref[rows, ps], jnp.concatenate([jnp.where(lo, kp, zero), jnp.where(lo, zero, kp)], axis=0)))
    for i, (ci, p) in enumerate(units):
        rows, ps = slice(ci * L, (ci + 1) * L), slice(p * LANE, (p + 1) * LANE)
        pm = (jnp.exp(jnp.where(causal, rowp_ref[ci, p:p + 1, :] - gp[i], -jnp.inf)) * qk[i]).astype(BF)
        pk_ref[ci, p, 0:L, :] = jnp.where(lo, pm, zero)
        pk_ref[ci, p, L:2 * L, :] = jnp.where(lo, zero, pm)
        qs = (q_ref[rows, ps].astype(F32) * jnp.exp(mp[i] - gp[i])).astype(BF)
        qs_ref[ci, p, 0:L, :] = jnp.where(lo, qs, zero)
        qs_ref[ci, p, L:2 * L, :] = jnp.where(lo, zero, qs)
    for i, (ci, p) in enumerate(units):
        rows, ps = slice(ci * L, (ci + 1) * L), slice(p * LANE, (p + 1) * LANE)
        kw = k_ref[rows, ps].astype(F32) * jnp.exp(wp[i])
        kw2 = jnp.concatenate([jnp.where(lo, kw, 0.0), jnp.where(lo, 0.0, kw)], axis=0)
        pk_ref[ci, p, 2 * L:4 * L, :] = kw2.T.astype(BF)


def _mlstm_local(q, k, colp, rowp, *, ts=512):
    b, s, nqk = q.shape
    nck = ts // CHUNK
    npair = C_HEADS // 2
    tok = lambda n: pl.BlockSpec((None, ts, n), lambda bi, i: (bi, i, 0))
    return pl.pallas_call(
        functools.partial(_mlstm_local_body, nck=nck),
        grid=(b, s // ts),
        in_specs=[tok(nqk), tok(nqk), tok(LANE),
                  pl.BlockSpec((None, nck, npair, LANE), lambda bi, i: (bi, i, 0, 0))],
        out_specs=[pl.BlockSpec((None, nck, npair, 4 * CHUNK, LANE), lambda bi, i: (bi, i, 0, 0, 0)),
                   pl.BlockSpec((None, nck, npair, 2 * CHUNK, LANE), lambda bi, i: (bi, i, 0, 0, 0)),
                   tok(LANE),
                   pl.BlockSpec((None, nck, C_HEADS, LANE), lambda bi, i: (bi, i, 0, 0))],
        out_shape=[jax.ShapeDtypeStruct((b, s // CHUNK, npair, 4 * CHUNK, LANE), BF),
                   jax.ShapeDtypeStruct((b, s // CHUNK, npair, 2 * CHUNK, LANE), BF),
                   jax.ShapeDtypeStruct((b, s, LANE), F32),
                   jax.ShapeDtypeStruct((b, s // CHUNK, C_HEADS, LANE), F32)],
        scratch_shapes=[pltpu.VMEM((C_HEADS, LANE), F32)],
        compiler_params=_params("parallel", "arbitrary"),
        name="mlstm_local",
    )(q, k, colp, rowp)


def _mlstm_scan_body(pk_ref, qs_ref, en_ref, se_ref, v_ref, og_ref, ong_ref,
                     y_ref, c_ref, *, nck, bb):
    L = CHUNK
    ones = jnp.ones((L, LANE), BF)

    @pl.when(pl.program_id(1) == 0)
    def _():
        c_ref[...] = jnp.zeros_like(c_ref)

    ch = [(bi, h) for bi in range(bb) for h in range(C_HEADS)]
    pairs = [(bi, p) for bi in range(bb) for p in range(C_HEADS // 2)]
    hsl = lambda h: slice(h * LANE, (h + 1) * LANE)
    srl = lambda h: slice((h % 2) * C_DQK, (h % 2 + 1) * C_DQK)
    for ci in range(nck):
        rows = slice(ci * L, (ci + 1) * L)
        qc = {(bi, p): _dot(qs_ref[bi, ci, p], c_ref[bi, p].astype(BF)) for bi, p in pairs}
        pv = {}
        for bi, p in pairs:
            vst = jnp.concatenate(
                [jnp.concatenate([v_ref[bi, rows, hsl(2 * p + sub)], ones], axis=1) for sub in range(2)], axis=0)
            pv[(bi, p)] = _dot(pk_ref[bi, ci, p], vst)
        for bi, h in ch:
            se = se_ref[bi, ci, h:h + 1, :]
            c_ref[bi, h // 2, srl(h), :] = (c_ref[bi, h // 2, srl(h), :] * jnp.concatenate([se, se], axis=1)
                                            + pv[(bi, h // 2)][2 * L + (h % 2) * L:3 * L + (h % 2) * L])
        res = [qc[(bi, h // 2)][srl(h)] + pv[(bi, h // 2)][srl(h)] for bi, h in ch]
        den = [jnp.maximum(jnp.abs(res[i][:, LANE:]), en_ref[bi, rows, _head_lane(h):_head_lane(h) + 1])
               for i, (bi, h) in enumerate(ch)]
        hh = [_sigmoid(og_ref[bi, rows, hsl(h)].astype(F32)) * (res[i][:, :LANE] / den[i])
              for i, (bi, h) in enumerate(ch)]
        for i, (bi, h) in enumerate(ch):
            y_ref[bi, rows, hsl(h)] = _rms(hh[i], ong_ref[:, hsl(h)]).astype(BF)


def _mlstm_scan(pk, qs, en, se, v, og, onorm_g, *, ts=256, bb=4):
    b, s, nv = v.shape
    bb = min(bb, b)
    nck = ts // CHUNK
    npair = C_HEADS // 2
    tok = lambda n: pl.BlockSpec((bb, ts, n), lambda bi, i: (bi, i, 0))
    return pl.pallas_call(
        functools.partial(_mlstm_scan_body, nck=nck, bb=bb),
        grid=(b // bb, s // ts),
        in_specs=[pl.BlockSpec((bb, nck, npair, 4 * CHUNK, LANE), lambda bi, i: (bi, i, 0, 0, 0)),
                  pl.BlockSpec((bb, nck, npair, 2 * CHUNK, LANE), lambda bi, i: (bi, i, 0, 0, 0)),
                  tok(LANE),
                  pl.BlockSpec((bb, nck, C_HEADS, LANE), lambda bi, i: (bi, i, 0, 0)),
                  tok(nv), tok(nv), pl.BlockSpec((1, nv), lambda bi, i: (0, 0))],
        out_specs=tok(nv),
        out_shape=jax.ShapeDtypeStruct((b, s, nv), BF),
        scratch_shapes=[pltpu.VMEM((bb, npair, LANE, 2 * LANE), F32)],
        compiler_params=_params("parallel", "arbitrary"),
        name="mlstm_scan",
    )(pk, qs, en, se, v, og, onorm_g.reshape(1, nv))


def kernel(x, norm_g, ffn1_gu, ffn1_down, ffn2_gu, ffn2_down, ev_w_in, ev_w_out, ev_a_ln_g, ev_a_ws, ev_a_bs, ev_b_conv, ev_b_a_log, ev_b_dt_bias, ev_b_onorm_g, od_w_in, od_w_out, od_b_i, od_b_f, od_onorm_g):
    b, s, d = x.shape
    x2 = x.reshape(b * s, d)
    depth = norm_g.shape[0]
    for layer in range(depth):
        ng = norm_g[layer]
        j = layer // 2
        x2 = _ffn(x2, ng[0], ffn1_gu[layer], ffn1_down[layer], ng[1])
        x3 = x2.reshape(b, s, d)
        if layer % 2 == 0:
            ya, q, k, v, gate, colp, rowp = _even_pre(
                x3, ng[2], ev_w_in[j], ev_a_ln_g[j], ev_a_ws[j], ev_a_bs[j], ev_b_conv[j],
                ev_b_a_log[j], ev_b_dt_bias[j])
            u, w, qg, attn, kdt, dec = _gdn_wy(q, k, v, colp, rowp)
            yb = _gdn_scan(u, w, qg, attn, kdt, dec, gate, ev_b_onorm_g[j])
            mix = (ya.reshape(b * s, -1), 0, yb.reshape(b * s, -1), 0, ev_w_out[j])
        else:
            q, k, v, og, colp, rowp = _odd_pre(x3, ng[2], od_w_in[j], od_b_i[j], od_b_f[j])
            pk, qs, en, se = _mlstm_local(q, k, colp, rowp)
            y = _mlstm_scan(pk, qs, en, se, v, og, od_onorm_g[j]).reshape(b * s, -1)
            mix = (y, 0, y, 1, od_w_out[j])
        x2 = _post_ffn(*mix, ng[3], x2, ng[4], ffn2_gu[layer], ffn2_down[layer], ng[5])
    return x2.reshape(b, s, d)
```

```python
import functools

import jax
import jax.numpy as jnp
from jax import lax
from jax.experimental import pallas as pl
from jax.experimental.pallas import tpu as pltpu

F32 = jnp.float32
BF = jnp.bfloat16
HI = lax.Precision.HIGHEST

EPS = 1e-6
LANE = 128
CHUNK = 64
A_GROUPS, A_CHUNK = 4, 128
B_HEADS, B_CONV = 4, 4
C_HEADS, C_DQK = 8, 64
GATE_CAP = 15.0
VMEM_LIMIT = 56 * 1024 * 1024


def _dot(a, b, prec=None):
    return jnp.dot(a, b, preferred_element_type=F32, precision=prec)


def _dot_nt(a, b, prec=None):
    return lax.dot_general(a, b, (((1,), (1,)), ((), ())), preferred_element_type=F32, precision=prec)


def _rms(xf, g):
    return xf * lax.rsqrt(jnp.mean(xf * xf, axis=-1, keepdims=True) + EPS) * g


def _sigmoid(x):
    return 1.0 / (1.0 + jnp.exp(-x))


def _silu(x):
    return x * _sigmoid(x)


def _gelu(x):
    return 0.5 * x * (1.0 + jnp.tanh(0.7978845608028654 * (x + 0.044715 * (x * x * x))))


def _softplus(x):
    return jnp.maximum(x, 0.0) + jnp.log1p(jnp.exp(-jnp.abs(x)))


def _chunk_tri(transpose):
    r = lax.broadcasted_iota(jnp.int32, (LANE, LANE), 1 if transpose else 0)
    c = lax.broadcasted_iota(jnp.int32, (LANE, LANE), 0 if transpose else 1)
    same = lax.shift_right_logical(r, 6) == lax.shift_right_logical(c, 6)
    return jnp.where(same, jnp.where(c <= r, 1.0, 0.0), 0.0).astype(BF)


def _split3(x):
    p1 = x.astype(BF).astype(F32)
    r1 = x - p1
    p2 = r1.astype(BF).astype(F32)
    return p1, p2, (r1 - p2).astype(BF).astype(F32)


SPLIT_LANES = 16


def _chunk_cumsum_cols(x):
    lane = lax.broadcasted_iota(jnp.int32, x.shape, 1)
    p1, p2, p3 = _split3(jnp.where(lane < SPLIT_LANES, x, 0.0))
    packed = (p1 + pltpu.roll(p2, SPLIT_LANES, axis=1) + pltpu.roll(p3, 2 * SPLIT_LANES, axis=1)).astype(BF)
    tri = _chunk_tri(False)
    res = jnp.concatenate([_dot(tri, packed[t:t + LANE]) for t in range(0, x.shape[0], LANE)], axis=0)
    return res + pltpu.roll(res, LANE - SPLIT_LANES, axis=1) + pltpu.roll(res, LANE - 2 * SPLIT_LANES, axis=1)


def _chunk_cumsum_rows(x):
    r = x.shape[0]
    p1, p2, p3 = _split3(x)
    stacked = jnp.concatenate([p1, p2, p3, jnp.zeros_like(p1)], axis=0).astype(BF)
    tri_t = _chunk_tri(True)
    res = jnp.concatenate([_dot(stacked[:, t:t + LANE], tri_t) for t in range(0, x.shape[1], LANE)], axis=1)
    return res[0:r] + res[r:2 * r] + res[2 * r:3 * r]


def _params(*sem):
    return pltpu.CompilerParams(dimension_semantics=sem, vmem_limit_bytes=VMEM_LIMIT)


def _vmem():
    return pl.BlockSpec(memory_space=pltpu.VMEM)


def _cast_body(w_ref, o_ref):
    o_ref[...] = w_ref[...].astype(BF)


def _to_bf16(w_stack, layer, *, steps=8):
    _, rows, cols = w_stack.shape
    tr = rows // steps
    return pl.pallas_call(
        _cast_body,
        grid=(steps,),
        in_specs=[pl.BlockSpec((None, tr, cols), lambda i: (layer, i, 0))],
        out_specs=pl.BlockSpec((tr, cols), lambda i: (i, 0)),
        out_shape=jax.ShapeDtypeStruct((rows, cols), BF),
        compiler_params=_params("parallel"),
        name="cast_bf16",
    )(w_stack)


def _swiglu_step(x, gpre_ref, wgu_ref, wd_ref, gpost_ref, o_ref, acc_ref, tf):
    xn = _rms(x, gpre_ref[...]).astype(BF)
    f = wd_ref.shape[0]

    def gate_up(j):
        return (_dot(xn, wgu_ref[:, j * tf:(j + 1) * tf]), _dot(xn, wgu_ref[:, f + j * tf:f + (j + 1) * tf]))

    n_chunks = f // tf
    gu = gate_up(0)
    for j in range(n_chunks):
        g, u = gu
        if j + 1 < n_chunks:
            gu = gate_up(j + 1)
        d = _dot((_silu(g) * u).astype(BF), wd_ref[j * tf:(j + 1) * tf, :])
        if j == 0:
            acc_ref[...] = d
        else:
            acc_ref[...] += d
    o_ref[...] = x + _rms(acc_ref[...], 0.5 * gpost_ref[...])


def _ffn_body(x_ref, gpre_ref, wgu_ref, wd_ref, gpost_ref, o_ref, acc_ref, *, tf):
    _swiglu_step(x_ref[...], gpre_ref, wgu_ref, wd_ref, gpost_ref, o_ref, acc_ref, tf)


def _ffn(x2, g_pre, w_gu, w_down, g_post, *, tm=512, tf=256):
    m, d = x2.shape
    row = pl.BlockSpec((tm, d), lambda i: (i, 0))
    vec = pl.BlockSpec((1, d), lambda i: (0, 0))
    return pl.pallas_call(
        functools.partial(_ffn_body, tf=tf),
        grid=(m // tm,),
        in_specs=[row, vec, _vmem(), _vmem(), vec],
        out_specs=row,
        out_shape=jax.ShapeDtypeStruct((m, d), F32),
        scratch_shapes=[pltpu.VMEM((tm, d), F32)],
        compiler_params=_params("parallel"),
        name="ffn",
    )(x2, g_pre.reshape(1, d), w_gu, w_down, g_post.reshape(1, d))


def _post_ffn_body(y1_ref, y2_ref, w1_ref, w2_ref, gmix_ref, x_ref, gpre_ref, wgu_ref, wd_ref, gpost_ref,
                   o_ref, acc_ref, *, tf):
    y = _dot(y1_ref[...], w1_ref[...]) + _dot(y2_ref[...], w2_ref[...])
    x = x_ref[...] + _rms(y, gmix_ref[...])
    _swiglu_step(x, gpre_ref, wgu_ref, wd_ref, gpost_ref, o_ref, acc_ref, tf)


def _post_ffn(y1, c1, y2, c2, w_out, g_mix, x2, g_pre, w_gu, w_down, g_post, *, tm=512, tf=256):
    m, d = x2.shape
    half = w_out.shape[0] // 2
    w = w_out.astype(BF)
    row = pl.BlockSpec((tm, d), lambda i: (i, 0))
    vec = pl.BlockSpec((1, d), lambda i: (0, 0))
    return pl.pallas_call(
        functools.partial(_post_ffn_body, tf=tf),
        grid=(m // tm,),
        in_specs=[pl.BlockSpec((tm, half), lambda i: (i, c1)), pl.BlockSpec((tm, half), lambda i: (i, c2)),
                  _vmem(), _vmem(), vec, row, vec, _vmem(), _vmem(), vec],
        out_specs=row,
        out_shape=jax.ShapeDtypeStruct((m, d), F32),
        scratch_shapes=[pltpu.VMEM((tm, d), F32)],
        compiler_params=_params("parallel"),
        name="post_ffn",
    )(y1, y2, w[:half], w[half:], g_mix.reshape(1, d), x2, g_pre.reshape(1, d),
      w_gu, w_down, g_post.reshape(1, d))


def _even_pre_body(x_ref, gpre_ref, wm_ref, wst_ref, lng_ref, aws_ref, bst_ref, cw_ref,
                   alog_c_ref, dtb_c_ref,
                   y_ref, q_ref, k_ref, v_ref, gate_ref, colp_ref, rowp_ref, zbuf_ref, *, ts):
    aw = A_GROUPS * LANE
    cq = 3 * B_HEADS * LANE
    xn = _rms(x_ref[...], gpre_ref[...]).astype(BF)

    zr = _dot_nt(wst_ref[...], xn)

    @pl.when(pl.program_id(1) == 0)
    def _():
        zbuf_ref[0:8, :] = jnp.zeros((8, cq), F32)

    zbuf_ref[8:8 + ts, :] = _dot(xn, wm_ref[:, 2 * aw:2 * aw + cq])
    zu = _dot(xn, wm_ref[:, 0:aw])
    zv = _dot(xn, wm_ref[:, aw:2 * aw])

    grow = -jnp.exp(alog_c_ref[...]) * _softplus(zr + dtb_c_ref[...])
    sub = lax.broadcasted_iota(jnp.int32, (8, ts), 0)
    gates = jnp.where(sub < B_HEADS, _sigmoid(zr), _chunk_cumsum_rows(grow))
    for ci in range(ts // CHUNK):
        rowp_ref[ci] = gates[:, ci * CHUNK:(ci + 1) * CHUNK]
    padded = jnp.concatenate([gates, jnp.zeros((LANE - 8, ts), F32)], axis=0)
    for t in range(0, ts, LANE):
        colp_ref[t:t + LANE, :] = padded[:, t:t + LANE].T
    gate_ref[...] = _dot(xn, wm_ref[:, 2 * aw + cq:]).astype(BF)

    u = _gelu(zu)
    v = _gelu(zv)
    r = lax.broadcasted_iota(jnp.int32, (A_CHUNK, A_CHUNK), 0)
    c = lax.broadcasted_iota(jnp.int32, (A_CHUNK, A_CHUNK), 1)
    for g in range(A_GROUPS):
        gs = slice(g * LANE, (g + 1) * LANE)
        vg = v[:, gs]
        d = vg - jnp.mean(vg, axis=-1, keepdims=True)
        var = jnp.mean(d * d, axis=-1, keepdims=True)
        vn = (d * lax.rsqrt(var + EPS) * lng_ref[:, gs]).astype(BF)
        w = jnp.where(r >= c, aws_ref[g], 0.0).astype(BF)
        bcol = bst_ref[:, g:g + 1]
        for ci in range(ts // A_CHUNK):
            rs = slice(ci * A_CHUNK, (ci + 1) * A_CHUNK)
            mixed = _dot(w, vn[rs]) + bcol
            y_ref[rs, gs] = (u[rs, gs] * mixed).astype(BF)

    for t in range(3 * B_HEADS):
        cs = slice(t * LANE, (t + 1) * LANE)
        zz = zbuf_ref[:, cs]
        acc = cw_ref[0:1, cs] * zz
        for j in range(1, B_CONV):
            acc = pltpu.roll(acc, 1, axis=0) + cw_ref[j:j + 1, cs] * zz
        a = _silu(acc[8:])
        if t < 2 * B_HEADS:
            a = a * lax.rsqrt(jnp.sum(a * a, axis=-1, keepdims=True) + EPS)
        if t < B_HEADS:
            q_ref[:, cs] = (a * (LANE ** -0.5)).astype(BF)
        elif t < 2 * B_HEADS:
            k_ref[:, slice((t - B_HEADS) * LANE, (t - B_HEADS + 1) * LANE)] = a.astype(BF)
        else:
            v_ref[:, slice((t - 2 * B_HEADS) * LANE, (t - 2 * B_HEADS + 1) * LANE)] = a.astype(BF)
    zbuf_ref[0:8, :] = zbuf_ref[ts:ts + 8, :]


def _even_pre(x3, g_pre, w_in, a_ln_g, a_ws, a_bs, b_conv, a_log, dt_bias, *, ts=512):
    b, s, d = x3.shape
    aw = A_GROUPS * LANE
    nq = 3 * B_HEADS * LANE
    o_beta = 2 * aw + nq
    wm = jnp.concatenate([w_in[:, :o_beta], w_in[:, o_beta + 2 * B_HEADS:]], axis=1).astype(BF)
    wst = w_in[:, o_beta:o_beta + 2 * B_HEADS].T.astype(BF)
    sub_pad = lambda p: jnp.pad(p, (B_HEADS, 0)).reshape(2 * B_HEADS, 1)
    grid = (b, s // ts)
    tok = lambda w: pl.BlockSpec((None, ts, w), lambda bi, i: (bi, i, 0))
    vec = lambda w: pl.BlockSpec((1, w), lambda bi, i: (0, 0))
    nck = ts // CHUNK
    outs = pl.pallas_call(
        functools.partial(_even_pre_body, ts=ts),
        grid=grid,
        in_specs=[tok(d), vec(d), _vmem(), _vmem(), vec(aw), _vmem(), _vmem(), _vmem(), _vmem(), _vmem()],
        out_specs=[tok(aw), tok(aw), tok(aw), tok(aw), tok(aw), tok(LANE),
                   pl.BlockSpec((None, nck, 8, CHUNK), lambda bi, i: (bi, i, 0, 0))],
        out_shape=[jax.ShapeDtypeStruct((b, s, aw), BF)] * 5
        + [jax.ShapeDtypeStruct((b, s, LANE), F32), jax.ShapeDtypeStruct((b, s // CHUNK, 8, CHUNK), F32)],
        scratch_shapes=[pltpu.VMEM((ts + 8, nq), F32)],
        compiler_params=_params("parallel", "arbitrary"),
        name="even_pre",
    )(x3, g_pre.reshape(1, d), wm, wst, a_ln_g.reshape(1, aw), a_ws, a_bs.T, b_conv,
      sub_pad(a_log), sub_pad(dt_bias))
    return outs


def _gdn_wy_body(q_ref, k_ref, v_ref, colp_ref, rowp_ref,
                 u_ref, w_ref, qg_ref, attn_ref, kdt_ref, dec_ref, *, nck, cg):
    L = CHUNK
    r = lax.broadcasted_iota(jnp.int32, (L, L), 0)
    c = lax.broadcasted_iota(jnp.int32, (L, L), 1)
    causal = r >= c
    strict = r > c
    eye = jnp.where(r == c, 1.0, 0.0).astype(F32)
    shr = lax.shift_right_logical
    diag8 = jnp.where(shr(r, 3) == shr(c, 3), 1.0, 0.0).astype(F32)
    off = [jnp.where((shr(r, t + 1) == shr(c, t + 1)) & (shr(r, t) > shr(c, t)), 1.0, 0.0).astype(F32)
           for t in (3, 4, 5)]

    def mm(a, b):
        return _dot(a.astype(BF), b.astype(BF))

    def group(units):
        n = range(len(units))
        rows = [slice(ci * L, (ci + 1) * L) for ci, _ in units]
        hs = [slice(h * LANE, (h + 1) * LANE) for _, h in units]
        kh = [k_ref[rows[i], hs[i]] for i in n]
        khf = [kh[i].astype(F32) for i in n]
        beta = [colp_ref[rows[i], h:h + 1] for i, (_, h) in enumerate(units)]
        gc = [colp_ref[rows[i], B_HEADS + h:B_HEADS + h + 1] for i, (_, h) in enumerate(units)]
        decay = [jnp.exp(jnp.where(causal, gc[i] - rowp_ref[ci, B_HEADS + h:B_HEADS + h + 1, :], -jnp.inf))
                 for i, (ci, h) in enumerate(units)]
        kb = [khf[i] * beta[i] for i in n]
        kk = [_dot_nt(kb[i].astype(BF), kh[i]) for i in n]
        qk = [_dot_nt(q_ref[rows[i], hs[i]], kh[i]) for i in n]
        low = [jnp.where(strict, kk[i] * decay[i], 0.0) for i in n]
        for i, (_, h) in enumerate(units):
            attn_ref[h, rows[i], :] = jnp.where(causal, qk[i] * decay[i], 0.0).astype(BF)
        pw = [-(low[i] * diag8) for i in n]
        inv = [eye + pw[i] for i in n]
        for _ in range(2):
            pw = [mm(pw[i], pw[i]) for i in n]
            t = [mm(inv[i], pw[i]) for i in n]
            inv = [inv[i] + t[i] for i in n]
        for m in off:
            t = [mm(low[i] * m, inv[i]) for i in n]
            t = [mm(inv[i], t[i]) for i in n]
            inv = [inv[i] - t[i] for i in n]
        eg = [jnp.exp(gc[i]) for i in n]
        sol = [mm(inv[i], jnp.concatenate([v_ref[rows[i], hs[i]].astype(F32) * beta[i], kb[i] * eg[i]], axis=1))
               for i in n]
        for i, (ci, h) in enumerate(units):
            u_ref[rows[i], hs[i]] = sol[i][:, :LANE]
            w_ref[rows[i], hs[i]] = sol[i][:, LANE:].astype(BF)
            qg_ref[rows[i], hs[i]] = (q_ref[rows[i], hs[i]].astype(F32) * eg[i]).astype(BF)
            gl = gc[i][L - 1:L, :]
            kd = khf[i] * jnp.exp(gl - gc[i])
            kdt = jnp.concatenate([kd, jnp.zeros_like(kd)], axis=0).T
            kdt_ref[ci, h] = kdt[:, :L].astype(BF)
            dec_ref[ci, h:h + 1, :] = jnp.broadcast_to(jnp.exp(gl), (1, LANE))

    for c0 in range(0, nck, cg):
        group([(ci, h) for ci in range(c0, c0 + cg) for h in range(B_HEADS)])


def _gdn_wy(q, k, v, colp, rowp, *, ts=512, cg=8):
    b, s, w = q.shape
    nck = ts // CHUNK
    grid = (b, s // ts)
    tok = lambda n: pl.BlockSpec((None, ts, n), lambda bi, i: (bi, i, 0))
    return pl.pallas_call(
        functools.partial(_gdn_wy_body, nck=nck, cg=cg),
        grid=grid,
        in_specs=[tok(w), tok(w), tok(w), tok(LANE),
                  pl.BlockSpec((None, nck, 8, CHUNK), lambda bi, i: (bi, i, 0, 0))],
        out_specs=[tok(w), tok(w), tok(w),
                   pl.BlockSpec((None, B_HEADS, ts, CHUNK), lambda bi, i: (bi, 0, i, 0)),
                   pl.BlockSpec((None, nck, B_HEADS, LANE, CHUNK), lambda bi, i: (bi, i, 0, 0, 0)),
                   pl.BlockSpec((None, nck, B_HEADS, LANE), lambda bi, i: (bi, i, 0, 0))],
        out_shape=[jax.ShapeDtypeStruct((b, s, w), F32), jax.ShapeDtypeStruct((b, s, w), BF),
                   jax.ShapeDtypeStruct((b, s, w), BF),
                   jax.ShapeDtypeStruct((b, B_HEADS, s, CHUNK), BF),
                   jax.ShapeDtypeStruct((b, s // CHUNK, B_HEADS, LANE, CHUNK), BF),
                   jax.ShapeDtypeStruct((b, s // CHUNK, B_HEADS, LANE), F32)],
        compiler_params=_params("parallel", "parallel"),
        name="gdn_wy",
    )(q, k, v, colp, rowp)


def _gdn_scan_body(u_ref, w_ref, qg_ref, attn_ref, kdt_ref, dec_ref, gate_ref, ong_ref,
                   y_ref, s_ref, *, nck, bb):
    L = CHUNK

    @pl.when(pl.program_id(1) == 0)
    def _():
        s_ref[...] = jnp.zeros_like(s_ref)

    ch = [(bi, h, slice(h * LANE, (h + 1) * LANE)) for bi in range(bb) for h in range(B_HEADS)]
    for ci in range(nck):
        rows = slice(ci * L, (ci + 1) * L)
        sb = [s_ref[bi, h].astype(BF) for bi, h, _ in ch]
        ws = [_dot(w_ref[bi, rows, hs], sb[i]) for i, (bi, _, hs) in enumerate(ch)]
        qs = [_dot(qg_ref[bi, rows, hs], sb[i]) for i, (bi, _, hs) in enumerate(ch)]
        vb = [(u_ref[bi, rows, hs] - ws[i]).astype(BF) for i, (bi, _, hs) in enumerate(ch)]
        upd = [_dot(kdt_ref[bi, ci, h], vb[i]) for i, (bi, h, _) in enumerate(ch)]
        av = [_dot(attn_ref[bi, h, rows, :], vb[i]) for i, (bi, h, _) in enumerate(ch)]
        for i, (bi, h, hs) in enumerate(ch):
            s_ref[bi, h] = s_ref[bi, h] * dec_ref[bi, ci, h:h + 1, :] + upd[i]
        for i, (bi, h, hs) in enumerate(ch):
            on = _rms(qs[i] + av[i], ong_ref[...])
            y_ref[bi, rows, hs] = (on * _silu(gate_ref[bi, rows, hs].astype(F32))).astype(BF)


def _gdn_scan(u, w, qg, attn, kdt, dec, gate, onorm_g, *, ts=256, bb=8):
    b, s, wd = u.shape
    bb = min(bb, b)
    nck = ts // CHUNK
    tok = lambda n: pl.BlockSpec((bb, ts, n), lambda bi, i: (bi, i, 0))
    return pl.pallas_call(
        functools.partial(_gdn_scan_body, nck=nck, bb=bb),
        grid=(b // bb, s // ts),
        in_specs=[tok(wd), tok(wd), tok(wd),
                  pl.BlockSpec((bb, B_HEADS, ts, CHUNK), lambda bi, i: (bi, 0, i, 0)),
                  pl.BlockSpec((bb, nck, B_HEADS, LANE, CHUNK), lambda bi, i: (bi, i, 0, 0, 0)),
                  pl.BlockSpec((bb, nck, B_HEADS, LANE), lambda bi, i: (bi, i, 0, 0)),
                  tok(wd), pl.BlockSpec((1, LANE), lambda bi, i: (0, 0))],
        out_specs=tok(wd),
        out_shape=jax.ShapeDtypeStruct((b, s, wd), BF),
        scratch_shapes=[pltpu.VMEM((bb, B_HEADS, LANE, LANE), F32)],
        compiler_params=_params("parallel", "arbitrary"),
        name="gdn_scan",
    )(u, w, qg, attn, kdt, dec, gate, onorm_g.reshape(1, LANE))


def _head_lane(h):
    return h // 2 + (C_HEADS // 2) * (h % 2)


def _odd_pre_body(x_ref, gpre_ref, wm_ref, wst_ref, bc_ref,
                  q_ref, k_ref, v_ref, og_ref, colp_ref, rowp_ref, *, ts):
    nqk = C_HEADS * C_DQK
    nv = C_HEADS * LANE
    xn = _rms(x_ref[...], gpre_ref[...]).astype(BF)
    zr = _dot_nt(wst_ref[...], xn)
    q_ref[...] = _dot(xn, wm_ref[:, 0:nqk]).astype(BF)
    k_ref[...] = (_dot(xn, wm_ref[:, nqk:2 * nqk]) * (C_DQK ** -0.5)).astype(BF)
    v_ref[...] = _dot(xn, wm_ref[:, 2 * nqk:2 * nqk + nv]).astype(BF)

    capr = GATE_CAP * jnp.tanh((zr + bc_ref[...]) * (1.0 / GATE_CAP))
    bcr = _chunk_cumsum_rows(-_softplus(-capr))
    og_ref[...] = _dot(xn, wm_ref[:, 2 * nqk + nv:]).astype(BF)
    sub = lax.broadcasted_iota(jnp.int32, (2 * C_HEADS, ts), 0)
    padded = jnp.concatenate([jnp.where(sub < C_HEADS, capr, bcr), jnp.zeros((LANE - 2 * C_HEADS, ts), F32)], axis=0)
    for t in range(0, ts, LANE):
        colp_ref[t:t + LANE, :] = padded[:, t:t + LANE].T
    brow = capr[0:C_HEADS] - bcr[C_HEADS:2 * C_HEADS]
    right = pltpu.roll(brow, CHUNK, axis=1)
    left = pltpu.roll(brow, ts - CHUNK, axis=1)
    npair = C_HEADS // 2
    lo = lax.broadcasted_iota(jnp.int32, (npair, LANE), 1) < CHUNK
    for ci in range(ts // CHUNK):
        tile = slice((ci // 2) * LANE, (ci // 2 + 1) * LANE)
        if ci % 2 == 0:
            rowp_ref[ci] = jnp.where(lo, brow[0:npair, tile], right[npair:C_HEADS, tile])
        else:
            rowp_ref[ci] = jnp.where(lo, left[0:npair, tile], brow[npair:C_HEADS, tile])


def _odd_pre(x3, g_pre, w_in, b_i, b_f, *, ts=512):
    b, s, d = x3.shape
    nqk = C_HEADS * C_DQK
    nv = C_HEADS * LANE
    o_g = 2 * nqk + nv
    wm = jnp.concatenate([w_in[:, :o_g], w_in[:, o_g + 2 * C_HEADS:]], axis=1).astype(BF)
    wsm = w_in[:, o_g:o_g + 2 * C_HEADS]
    bias = jnp.concatenate([b_i, b_f])
    order = jnp.array([g * C_HEADS + h for g in range(2) for h in (*range(0, C_HEADS, 2), *range(1, C_HEADS, 2))])
    wst = wsm.T[order].astype(BF)
    tok = lambda n: pl.BlockSpec((None, ts, n), lambda bi, i: (bi, i, 0))
    vec = lambda n: pl.BlockSpec((1, n), lambda bi, i: (0, 0))
    nck = ts // CHUNK
    return pl.pallas_call(
        functools.partial(_odd_pre_body, ts=ts),
        grid=(b, s // ts),
        in_specs=[tok(d), vec(d), _vmem(), _vmem(), _vmem()],
        out_specs=[tok(nqk), tok(nqk), tok(nv), tok(nv), tok(LANE),
                   pl.BlockSpec((None, nck, C_HEADS // 2, LANE), lambda bi, i: (bi, i, 0, 0))],
        out_shape=[jax.ShapeDtypeStruct((b, s, nqk), BF), jax.ShapeDtypeStruct((b, s, nqk), BF),
                   jax.ShapeDtypeStruct((b, s, nv), BF), jax.ShapeDtypeStruct((b, s, nv), BF),
                   jax.ShapeDtypeStruct((b, s, LANE), F32),
                   jax.ShapeDtypeStruct((b, s // CHUNK, C_HEADS // 2, LANE), F32)],
        compiler_params=_params("parallel", "parallel"),
        name="odd_pre",
    )(x3, g_pre.reshape(1, d), wm, wst, bias[order].reshape(2 * C_HEADS, 1))


def _mlstm_local_body(q_ref, k_ref, colp_ref, rowp_ref,
                      pk_ref, qs_ref, en_ref, se_ref, m_ref, *, nck):
    L = CHUNK
    npair = C_HEADS // 2
    row = lax.broadcasted_iota(jnp.int32, (L, LANE), 0)
    lane = lax.broadcasted_iota(jnp.int32, (L, LANE), 1)
    lo = lane < C_DQK
    causal = row >= (lane & (L - 1))
    spread = [jnp.where(lo, _head_lane(2 * p), _head_lane(2 * p + 1)) for p in range(npair)]
    zero = jnp.zeros((L, LANE), BF)

    @pl.when(pl.program_id(1) == 0)
    def _():
        m_ref[...] = jnp.zeros_like(m_ref)

    m = m_ref[0:1, :]
    g, m_in, wsc = [], [], []
    for ci in range(nck):
        rows = slice(ci * L, (ci + 1) * L)
        colp = colp_ref[rows, :]
        li = jnp.where(lane < C_HEADS, colp, 0.0)
        bc = jnp.where(lane < C_HEADS, pltpu.roll(colp, LANE - C_HEADS, axis=1), 0.0)
        cm = li - bc
        for s in (1, 2, 4, 8, 16, 32):
            cm = jnp.maximum(cm, jnp.where(row >= s, pltpu.roll(cm, s, axis=0), -jnp.inf))
        gi = jnp.maximum(m, cm)
        en_ref[rows, :] = jnp.exp(-bc - gi)
        bl = bc[L - 1:L, :]
        we = bl - bc + li
        m_new = jnp.maximum(bl + m, jnp.max(we, axis=0, keepdims=True))
        sev = jnp.exp(bl + m - m_new)
        for h in range(C_HEADS):
            se_ref[ci, h:h + 1, :] = jnp.broadcast_to(sev[:, _head_lane(h):_head_lane(h) + 1], (1, LANE))
        g.append(gi)
        m_in.append(jnp.broadcast_to(m, (8, LANE)))
        wsc.append(we - m_new)
        m = m_new
    m_ref[...] = jnp.broadcast_to(m, m_ref.shape)

    units = [(ci, p) for ci in range(nck) for p in range(npair)]
    gp = [jnp.take_along_axis(g[ci], spread[p], axis=1) for ci, p in units]
    mp = [jnp.take_along_axis(m_in[ci], spread[p][0:8], axis=1)[0:1] for ci, p in units]
    wp = [jnp.take_along_axis(wsc[ci], spread[p], axis=1) for ci, p in units]

    qk = []
    for ci, p in units:
        rows, ps = slice(ci * L, (ci + 1) * L), slice(p * LANE, (p + 1) * LANE)
        kp = k_ref[rows, ps]
        qk.append(_dot_nt(q_ref[rows, ps], jnp.concatenate([jnp.where(lo, kp, zero), jnp.where(lo, zero, kp)], axis=0)))
    for i, (ci, p) in enumerate(units):
        rows, ps = slice(ci * L, (ci + 1) * L), slice(p * LANE, (p + 1) * LANE)
        pm = (jnp.exp(jnp.where(causal, rowp_ref[ci, p:p + 1, :] - gp[i], -jnp.inf)) * qk[i]).astype(BF)
        pk_ref[ci, p, 0:L, :] = jnp.where(lo, pm, zero)
        pk_ref[ci, p, L:2 * L, :] = jnp.where(lo, zero, pm)
        qs = (q_ref[rows, ps].astype(F32) * jnp.exp(mp[i] - gp[i])).astype(BF)
        qs_ref[ci, p, 0:L, :] = jnp.where(lo, qs, zero)
        qs_ref[ci, p, L:2 * L, :] = jnp.where(lo, zero, qs)
    for i, (ci, p) in enumerate(units):
        rows, ps = slice(ci * L, (ci + 1) * L), slice(p * LANE, (p + 1) * LANE)
        kw = k_ref[rows, ps].astype(F32) * jnp.exp(wp[i])
        kw2 = jnp.concatenate([jnp.where(lo, kw, 0.0), jnp.where(lo, 0.0, kw)], axis=0)
        pk_ref[ci, p, 2 * L:4 * L, :] = kw2.T.astype(BF)


def _mlstm_local(q, k, colp, rowp, *, ts=512):
    b, s, nqk = q.shape
    nck = ts // CHUNK
    npair = C_HEADS // 2
    tok = lambda n: pl.BlockSpec((None, ts, n), lambda bi, i: (bi, i, 0))
    return pl.pallas_call(
        functools.partial(_mlstm_local_body, nck=nck),
        grid=(b, s // ts),
        in_specs=[tok(nqk), tok(nqk), tok(LANE),
                  pl.BlockSpec((None, nck, npair, LANE), lambda bi, i: (bi, i, 0, 0))],
        out_specs=[pl.BlockSpec((None, nck, npair, 4 * CHUNK, LANE), lambda bi, i: (bi, i, 0, 0, 0)),
                   pl.BlockSpec((None, nck, npair, 2 * CHUNK, LANE), lambda bi, i: (bi, i, 0, 0, 0)),
                   tok(LANE),
                   pl.BlockSpec((None, nck, C_HEADS, LANE), lambda bi, i: (bi, i, 0, 0))],
        out_shape=[jax.ShapeDtypeStruct((b, s // CHUNK, npair, 4 * CHUNK, LANE), BF),
                   jax.ShapeDtypeStruct((b, s // CHUNK, npair, 2 * CHUNK, LANE), BF),
                   jax.ShapeDtypeStruct((b, s, LANE), F32),
                   jax.ShapeDtypeStruct((b, s // CHUNK, C_HEADS, LANE), F32)],
        scratch_shapes=[pltpu.VMEM((C_HEADS, LANE), F32)],
        compiler_params=_params("parallel", "arbitrary"),
        name="mlstm_local",
    )(q, k, colp, rowp)


def _mlstm_scan_body(pk_ref, qs_ref, en_ref, se_ref, v_ref, og_ref, ong_ref,
                     y_ref, c_ref, *, nck, bb):
    L = CHUNK
    ones = jnp.ones((L, LANE), BF)

    @pl.when(pl.program_id(1) == 0)
    def _():
        c_ref[...] = jnp.zeros_like(c_ref)

    ch = [(bi, h) for bi in range(bb) for h in range(C_HEADS)]
    pairs = [(bi, p) for bi in range(bb) for p in range(C_HEADS // 2)]
    hsl = lambda h: slice(h * LANE, (h + 1) * LANE)
    srl = lambda h: slice((h % 2) * C_DQK, (h % 2 + 1) * C_DQK)
    for ci in range(nck):
        rows = slice(ci * L, (ci + 1) * L)
        qc = {(bi, p): _dot(qs_ref[bi, ci, p], c_ref[bi, p].astype(BF)) for bi, p in pairs}
        pv = {}
        for bi, p in pairs:
            vst = jnp.concatenate(
                [jnp.concatenate([v_ref[bi, rows, hsl(2 * p + sub)], ones], axis=1) for sub in range(2)], axis=0)
            pv[(bi, p)] = _dot(pk_ref[bi, ci, p], vst)
        for bi, h in ch:
            se = se_ref[bi, ci, h:h + 1, :]
            c_ref[bi, h // 2, srl(h), :] = (c_ref[bi, h // 2, srl(h), :] * jnp.concatenate([se, se], axis=1)
                                            + pv[(bi, h // 2)][2 * L + (h % 2) * L:3 * L + (h % 2) * L])
        res = [qc[(bi, h // 2)][srl(h)] + pv[(bi, h // 2)][srl(h)] for bi, h in ch]
        den = [jnp.maximum(jnp.abs(res[i][:, LANE:]), en_ref[bi, rows, _head_lane(h):_head_lane(h) + 1])
               for i, (bi, h) in enumerate(ch)]
        hh = [_sigmoid(og_ref[bi, rows, hsl(h)].astype(F32)) * (res[i][:, :LANE] / den[i])
              for i, (bi, h) in enumerate(ch)]
        for i, (bi, h) in enumerate(ch):
            y_ref[bi, rows, hsl(h)] = _rms(hh[i], ong_ref[:, hsl(h)]).astype(BF)


def _mlstm_scan(pk, qs, en, se, v, og, onorm_g, *, ts=256, bb=4):
    b, s, nv = v.shape
    bb = min(bb, b)
    nck = ts // CHUNK
    npair = C_HEADS // 2
    tok = lambda n: pl.BlockSpec((bb, ts, n), lambda bi, i: (bi, i, 0))
    return pl.pallas_call(
        functools.partial(_mlstm_scan_body, nck=nck, bb=bb),
        grid=(b // bb, s // ts),
        in_specs=[pl.BlockSpec((bb, nck, npair, 4 * CHUNK, LANE), lambda bi, i: (bi, i, 0, 0, 0)),
                  pl.BlockSpec((bb, nck, npair, 2 * CHUNK, LANE), lambda bi, i: (bi, i, 0, 0, 0)),
                  tok(LANE),
                  pl.BlockSpec((bb, nck, C_HEADS, LANE), lambda bi, i: (bi, i, 0, 0)),
                  tok(nv), tok(nv), pl.BlockSpec((1, nv), lambda bi, i: (0, 0))],
        out_specs=tok(nv),
        out_shape=jax.ShapeDtypeStruct((b, s, nv), BF),
        scratch_shapes=[pltpu.VMEM((bb, npair, LANE, 2 * LANE), F32)],
        compiler_params=_params("parallel", "arbitrary"),
        name="mlstm_scan",
    )(pk, qs, en, se, v, og, onorm_g.reshape(1, nv))


def kernel(x, norm_g, ffn1_gu, ffn1_down, ffn2_gu, ffn2_down, ev_w_in, ev_w_out, ev_a_ln_g, ev_a_ws, ev_a_bs, ev_b_conv, ev_b_a_log, ev_b_dt_bias, ev_b_onorm_g, od_w_in, od_w_out, od_b_i, od_b_f, od_onorm_g):
    b, s, d = x.shape
    x2 = x.reshape(b * s, d)
    depth = norm_g.shape[0]
    for layer in range(depth):
        ng = norm_g[layer]
        j = layer // 2
        x2 = _ffn(x2, ng[0], _to_bf16(ffn1_gu, layer), _to_bf16(ffn1_down, layer), ng[1])
        x3 = x2.reshape(b, s, d)
        if layer % 2 == 0:
            ya, q, k, v, gate, colp, rowp = _even_pre(
                x3, ng[2], ev_w_in[j], ev_a_ln_g[j], ev_a_ws[j], ev_a_bs[j], ev_b_conv[j],
                ev_b_a_log[j], ev_b_dt_bias[j])
            u, w, qg, attn, kdt, dec = _gdn_wy(q, k, v, colp, rowp)
            yb = _gdn_scan(u, w, qg, attn, kdt, dec, gate, ev_b_onorm_g[j])
            mix = (ya.reshape(b * s, -1), 0, yb.reshape(b * s, -1), 0, ev_w_out[j])
        else:
            q, k, v, og, colp, rowp = _odd_pre(x3, ng[2], od_w_in[j], od_b_i[j], od_b_f[j])
            pk, qs, en, se = _mlstm_local(q, k, colp, rowp)
            y = _mlstm_scan(pk, qs, en, se, v, og, od_onorm_g[j]).reshape(b * s, -1)
            mix = (y, 0, y, 1, od_w_out[j])
        x2 = _post_ffn(*mix, ng[3], x2, ng[4], _to_bf16(ffn2_gu, layer), _to_bf16(ffn2_down, layer), ng[5])
    return x2.reshape(b, s, d)
```

```python
import functools

import jax
import jax.numpy as jnp
from jax import lax
from jax.experimental import pallas as pl
from jax.experimental.pallas import tpu as pltpu

F32 = jnp.float32
BF = jnp.bfloat16
HI = lax.Precision.HIGHEST

EPS = 1e-6
LANE = 128
CHUNK = 64
A_GROUPS, A_CHUNK = 4, 128
B_HEADS, B_CONV = 4, 4
C_HEADS, C_DQK = 8, 64
GATE_CAP = 15.0
VMEM_LIMIT = 56 * 1024 * 1024


def _dot(a, b, prec=None):
    return jnp.dot(a, b, preferred_element_type=F32, precision=prec)


def _dot_nt(a, b, prec=None):
    return lax.dot_general(a, b, (((1,), (1,)), ((), ())), preferred_element_type=F32, precision=prec)


def _rms(xf, g):
    return xf * lax.rsqrt(jnp.mean(xf * xf, axis=-1, keepdims=True) + EPS) * g


def _sigmoid(x):
    return 1.0 / (1.0 + jnp.exp(-x))


def _silu(x):
    return x * _sigmoid(x)


def _gelu(x):
    return 0.5 * x * (1.0 + jnp.tanh(0.7978845608028654 * (x + 0.044715 * (x * x * x))))


def _softplus(x):
    return jnp.maximum(x, 0.0) + jnp.log1p(jnp.exp(-jnp.abs(x)))


def _chunk_tri(transpose):
    r = lax.broadcasted_iota(jnp.int32, (LANE, LANE), 1 if transpose else 0)
    c = lax.broadcasted_iota(jnp.int32, (LANE, LANE), 0 if transpose else 1)
    same = lax.shift_right_logical(r, 6) == lax.shift_right_logical(c, 6)
    return jnp.where(same, jnp.where(c <= r, 1.0, 0.0), 0.0).astype(BF)


def _split3(x):
    p1 = x.astype(BF).astype(F32)
    r1 = x - p1
    p2 = r1.astype(BF).astype(F32)
    return p1, p2, (r1 - p2).astype(BF).astype(F32)


SPLIT_LANES = 16


def _chunk_cumsum_cols(x):
    lane = lax.broadcasted_iota(jnp.int32, x.shape, 1)
    p1, p2, p3 = _split3(jnp.where(lane < SPLIT_LANES, x, 0.0))
    packed = (p1 + pltpu.roll(p2, SPLIT_LANES, axis=1) + pltpu.roll(p3, 2 * SPLIT_LANES, axis=1)).astype(BF)
    tri = _chunk_tri(False)
    res = jnp.concatenate([_dot(tri, packed[t:t + LANE]) for t in range(0, x.shape[0], LANE)], axis=0)
    return res + pltpu.roll(res, LANE - SPLIT_LANES, axis=1) + pltpu.roll(res, LANE - 2 * SPLIT_LANES, axis=1)


def _chunk_cumsum_rows(x):
    r = x.shape[0]
    p1, p2, p3 = _split3(x)
    stacked = jnp.concatenate([p1, p2, p3, jnp.zeros_like(p1)], axis=0).astype(BF)
    tri_t = _chunk_tri(True)
    res = jnp.concatenate([_dot(stacked[:, t:t + LANE], tri_t) for t in range(0, x.shape[1], LANE)], axis=1)
    return res[0:r] + res[r:2 * r] + res[2 * r:3 * r]


def _params(*sem):
    return pltpu.CompilerParams(dimension_semantics=sem, vmem_limit_bytes=VMEM_LIMIT)


def _vmem():
    return pl.BlockSpec(memory_space=pltpu.VMEM)


def _cast_body(w_ref, o_ref):
    o_ref[...] = w_ref[...].astype(BF)


def _to_bf16(w_stack, layer, *, steps=8):
    _, rows, cols = w_stack.shape
    tr = rows // steps
    return pl.pallas_call(
        _cast_body,
        grid=(steps,),
        in_specs=[pl.BlockSpec((None, tr, cols), lambda i: (layer, i, 0))],
        out_specs=pl.BlockSpec((tr, cols), lambda i: (i, 0)),
        out_shape=jax.ShapeDtypeStruct((rows, cols), BF),
        compiler_params=_params("parallel"),
        name="cast_bf16",
    )(w_stack)


def _swiglu_step(x, gpre_ref, wgu_ref, wd_ref, gpost_ref, o_ref, acc_ref, tf):
    xn = _rms(x, gpre_ref[...]).astype(BF)
    f = wd_ref.shape[0]

    def gate_up(j):
        return (_dot(xn, wgu_ref[:, j * tf:(j + 1) * tf]), _dot(xn, wgu_ref[:, f + j * tf:f + (j + 1) * tf]))

    n_chunks = f // tf
    gu = gate_up(0)
    for j in range(n_chunks):
        g, u = gu
        if j + 1 < n_chunks:
            gu = gate_up(j + 1)
        d = _dot((_silu(g) * u).astype(BF), wd_ref[j * tf:(j + 1) * tf, :])
        if j == 0:
            acc_ref[...] = d
        else:
            acc_ref[...] += d
    o_ref[...] = x + _rms(acc_ref[...], 0.5 * gpost_ref[...])


def _ffn_body(x_ref, gpre_ref, wgu_ref, wd_ref, gpost_ref, o_ref, acc_ref, *, tf):
    _swiglu_step(x_ref[...], gpre_ref, wgu_ref, wd_ref, gpost_ref, o_ref, acc_ref, tf)


def _ffn(x2, g_pre, w_gu, w_down, g_post, *, tm=512, tf=256):
    m, d = x2.shape
    row = pl.BlockSpec((tm, d), lambda i: (i, 0))
    vec = pl.BlockSpec((1, d), lambda i: (0, 0))
    return pl.pallas_call(
        functools.partial(_ffn_body, tf=tf),
        grid=(m // tm,),
        in_specs=[row, vec, _vmem(), _vmem(), vec],
        out_specs=row,
        out_shape=jax.ShapeDtypeStruct((m, d), F32),
        scratch_shapes=[pltpu.VMEM((tm, d), F32)],
        compiler_params=_params("parallel"),
        name="ffn",
    )(x2, g_pre.reshape(1, d), w_gu, w_down, g_post.reshape(1, d))


def _post_ffn_body(y1_ref, y2_ref, w1_ref, w2_ref, gmix_ref, x_ref, gpre_ref, wgu_ref, wd_ref, gpost_ref,
                   o_ref, acc_ref, *, tf):
    y = _dot(y1_ref[...], w1_ref[...]) + _dot(y2_ref[...], w2_ref[...])
    x = x_ref[...] + _rms(y, gmix_ref[...])
    _swiglu_step(x, gpre_ref, wgu_ref, wd_ref, gpost_ref, o_ref, acc_ref, tf)


def _post_ffn(y1, c1, y2, c2, w_out, g_mix, x2, g_pre, w_gu, w_down, g_post, *, tm=512, tf=256):
    m, d = x2.shape
    half = w_out.shape[0] // 2
    w = w_out.astype(BF)
    row = pl.BlockSpec((tm, d), lambda i: (i, 0))
    vec = pl.BlockSpec((1, d), lambda i: (0, 0))
    return pl.pallas_call(
        functools.partial(_post_ffn_body, tf=tf),
        grid=(m // tm,),
        in_specs=[pl.BlockSpec((tm, half), lambda i: (i, c1)), pl.BlockSpec((tm, half), lambda i: (i, c2)),
                  _vmem(), _vmem(), vec, row, vec, _vmem(), _vmem(), vec],
        out_specs=row,
        out_shape=jax.ShapeDtypeStruct((m, d), F32),
        scratch_shapes=[pltpu.VMEM((tm, d), F32)],
        compiler_params=_params("parallel"),
        name="post_ffn",
    )(y1, y2, w[:half], w[half:], g_mix.reshape(1, d), x2, g_pre.reshape(1, d),
      w_gu, w_down, g_post.reshape(1, d))


def _even_pre_body(x_ref, gpre_ref, wm_ref, wst_ref, lng_ref, aws_ref, bst_ref, cw_ref,
                   alog_c_ref, dtb_c_ref,
                   y_ref, q_ref, k_ref, v_ref, gate_ref, colp_ref, rowp_ref, zbuf_ref, *, ts):
    aw = A_GROUPS * LANE
    cq = 3 * B_HEADS * LANE
    xn = _rms(x_ref[...], gpre_ref[...]).astype(BF)

    zr = _dot_nt(wst_ref[...], xn)

    @pl.when(pl.program_id(1) == 0)
    def _():
        zbuf_ref[0:8, :] = jnp.zeros((8, cq), F32)

    zbuf_ref[8:8 + ts, :] = _dot(xn, wm_ref[:, 2 * aw:2 * aw + cq])
    zu = _dot(xn, wm_ref[:, 0:aw])
    zv = _dot(xn, wm_ref[:, aw:2 * aw])

    grow = -jnp.exp(alog_c_ref[...]) * _softplus(zr + dtb_c_ref[...])
    sub = lax.broadcasted_iota(jnp.int32, (8, ts), 0)
    gates = jnp.where(sub < B_HEADS, _sigmoid(zr), _chunk_cumsum_rows(grow))
    for ci in range(ts // CHUNK):
        rowp_ref[ci] = gates[:, ci * CHUNK:(ci + 1) * CHUNK]
    padded = jnp.concatenate([gates, jnp.zeros((LANE - 8, ts), F32)], axis=0)
    for t in range(0, ts, LANE):
        colp_ref[t:t + LANE, :] = padded[:, t:t + LANE].T
    gate_ref[...] = _dot(xn, wm_ref[:, 2 * aw + cq:]).astype(BF)

    u = _gelu(zu)
    v = _gelu(zv)
    r = lax.broadcasted_iota(jnp.int32, (A_CHUNK, A_CHUNK), 0)
    c = lax.broadcasted_iota(jnp.int32, (A_CHUNK, A_CHUNK), 1)
    for g in range(A_GROUPS):
        gs = slice(g * LANE, (g + 1) * LANE)
        vg = v[:, gs]
        d = vg - jnp.mean(vg, axis=-1, keepdims=True)
        var = jnp.mean(d * d, axis=-1, keepdims=True)
        vn = (d * lax.rsqrt(var + EPS) * lng_ref[:, gs]).astype(BF)
        w = jnp.where(r >= c, aws_ref[g], 0.0).astype(BF)
        bcol = bst_ref[:, g:g + 1]
        for ci in range(ts // A_CHUNK):
            rs = slice(ci * A_CHUNK, (ci + 1) * A_CHUNK)
            mixed = _dot(w, vn[rs]) + bcol
            y_ref[rs, gs] = (u[rs, gs] * mixed).astype(BF)

    for t in range(3 * B_HEADS):
        cs = slice(t * LANE, (t + 1) * LANE)
        zz = zbuf_ref[:, cs]
        sz = pltpu.roll(zz, 1, axis=0)
        near = cw_ref[3:4, cs] * zz + cw_ref[2:3, cs] * sz
        far = cw_ref[1:2, cs] * zz + cw_ref[0:1, cs] * sz
        a = _silu((near + pltpu.roll(far, 2, axis=0))[8:])
        if t < 2 * B_HEADS:
            a = a * lax.rsqrt(jnp.sum(a * a, axis=-1, keepdims=True) + EPS)
        if t < B_HEADS:
            q_ref[:, cs] = (a * (LANE ** -0.5)).astype(BF)
        elif t < 2 * B_HEADS:
            k_ref[:, slice((t - B_HEADS) * LANE, (t - B_HEADS + 1) * LANE)] = a.astype(BF)
        else:
            v_ref[:, slice((t - 2 * B_HEADS) * LANE, (t - 2 * B_HEADS + 1) * LANE)] = a.astype(BF)
    zbuf_ref[0:8, :] = zbuf_ref[ts:ts + 8, :]


def _even_pre(x3, g_pre, w_in, a_ln_g, a_ws, a_bs, b_conv, a_log, dt_bias, *, ts=512):
    assert B_CONV == 4 and b_conv.shape[0] == B_CONV
    b, s, d = x3.shape
    aw = A_GROUPS * LANE
    nq = 3 * B_HEADS * LANE
    o_beta = 2 * aw + nq
    wm = jnp.concatenate([w_in[:, :o_beta], w_in[:, o_beta + 2 * B_HEADS:]], axis=1).astype(BF)
    wst = w_in[:, o_beta:o_beta + 2 * B_HEADS].T.astype(BF)
    sub_pad = lambda p: jnp.pad(p, (B_HEADS, 0)).reshape(2 * B_HEADS, 1)
    grid = (b, s // ts)
    tok = lambda w: pl.BlockSpec((None, ts, w), lambda bi, i: (bi, i, 0))
    vec = lambda w: pl.BlockSpec((1, w), lambda bi, i: (0, 0))
    nck = ts // CHUNK
    outs = pl.pallas_call(
        functools.partial(_even_pre_body, ts=ts),
        grid=grid,
        in_specs=[tok(d), vec(d), _vmem(), _vmem(), vec(aw), _vmem(), _vmem(), _vmem(), _vmem(), _vmem()],
        out_specs=[tok(aw), tok(aw), tok(aw), tok(aw), tok(aw), tok(LANE),
                   pl.BlockSpec((None, nck, 8, CHUNK), lambda bi, i: (bi, i, 0, 0))],
        out_shape=[jax.ShapeDtypeStruct((b, s, aw), BF)] * 5
        + [jax.ShapeDtypeStruct((b, s, LANE), F32), jax.ShapeDtypeStruct((b, s // CHUNK, 8, CHUNK), F32)],
        scratch_shapes=[pltpu.VMEM((ts + 8, nq), F32)],
        compiler_params=_params("parallel", "arbitrary"),
        name="even_pre",
    )(x3, g_pre.reshape(1, d), wm, wst, a_ln_g.reshape(1, aw), a_ws, a_bs.T, b_conv,
      sub_pad(a_log), sub_pad(dt_bias))
    return outs


def _gdn_wy_body(q_ref, k_ref, v_ref, colp_ref, rowp_ref,
                 u_ref, w_ref, qg_ref, attn_ref, kdt_ref, dec_ref, *, nck, cg):
    L = CHUNK
    r = lax.broadcasted_iota(jnp.int32, (L, L), 0)
    c = lax.broadcasted_iota(jnp.int32, (L, L), 1)
    causal = r >= c
    strict = r > c
    eye = jnp.where(r == c, 1.0, 0.0).astype(F32)
    shr = lax.shift_right_logical
    diag8 = jnp.where(shr(r, 3) == shr(c, 3), 1.0, 0.0).astype(F32)
    off = [jnp.where((shr(r, t + 1) == shr(c, t + 1)) & (shr(r, t) > shr(c, t)), 1.0, 0.0).astype(F32)
           for t in (3, 4, 5)]

    def mm(a, b):
        return _dot(a.astype(BF), b.astype(BF))

    def group(units):
        n = range(len(units))
        rows = [slice(ci * L, (ci + 1) * L) for ci, _ in units]
        hs = [slice(h * LANE, (h + 1) * LANE) for _, h in units]
        kh = [k_ref[rows[i], hs[i]] for i in n]
        khf = [kh[i].astype(F32) for i in n]
        beta = [colp_ref[rows[i], h:h + 1] for i, (_, h) in enumerate(units)]
        gc = [colp_ref[rows[i], B_HEADS + h:B_HEADS + h + 1] for i, (_, h) in enumerate(units)]
        decay = [jnp.exp(jnp.where(causal, gc[i] - rowp_ref[ci, B_HEADS + h:B_HEADS + h + 1, :], -jnp.inf))
                 for i, (ci, h) in enumerate(units)]
        kb = [khf[i] * beta[i] for i in n]
        kk = [_dot_nt(kb[i].astype(BF), kh[i]) for i in n]
        qk = [_dot_nt(q_ref[rows[i], hs[i]], kh[i]) for i in n]
        low = [jnp.where(strict, kk[i] * decay[i], 0.0) for i in n]
        for i, (_, h) in enumerate(units):
            attn_ref[h, rows[i], :] = jnp.where(causal, qk[i] * decay[i], 0.0).astype(BF)
        pw = [-(low[i] * diag8) for i in n]
        inv = [eye + pw[i] for i in n]
        for _ in range(2):
            pw = [mm(pw[i], pw[i]) for i in n]
            t = [mm(inv[i], pw[i]) for i in n]
            inv = [inv[i] + t[i] for i in n]
        for m in off:
            t = [mm(low[i] * m, inv[i]) for i in n]
            t = [mm(inv[i], t[i]) for i in n]
            inv = [inv[i] - t[i] for i in n]
        eg = [jnp.exp(gc[i]) for i in n]
        sol = [mm(inv[i], jnp.concatenate([v_ref[rows[i], hs[i]].astype(F32) * beta[i], kb[i] * eg[i]], axis=1))
               for i in n]
        for i, (ci, h) in enumerate(units):
            u_ref[rows[i], hs[i]] = sol[i][:, :LANE]
            w_ref[rows[i], hs[i]] = sol[i][:, LANE:].astype(BF)
            qg_ref[rows[i], hs[i]] = (q_ref[rows[i], hs[i]].astype(F32) * eg[i]).astype(BF)
            gl = gc[i][L - 1:L, :]
            kd = khf[i] * jnp.exp(gl - gc[i])
            kdt = jnp.concatenate([kd, jnp.zeros_like(kd)], axis=0).T
            kdt_ref[ci, h] = kdt[:, :L].astype(BF)
            dec_ref[ci, h:h + 1, :] = jnp.broadcast_to(jnp.exp(gl), (1, LANE))

    for c0 in range(0, nck, cg):
        group([(ci, h) for ci in range(c0, c0 + cg) for h in range(B_HEADS)])


def _gdn_wy(q, k, v, colp, rowp, *, ts=512, cg=8):
    b, s, w = q.shape
    nck = ts // CHUNK
    grid = (b, s // ts)
    tok = lambda n: pl.BlockSpec((None, ts, n), lambda bi, i: (bi, i, 0))
    return pl.pallas_call(
        functools.partial(_gdn_wy_body, nck=nck, cg=cg),
        grid=grid,
        in_specs=[tok(w), tok(w), tok(w), tok(LANE),
                  pl.BlockSpec((None, nck, 8, CHUNK), lambda bi, i: (bi, i, 0, 0))],
        out_specs=[tok(w), tok(w), tok(w),
                   pl.BlockSpec((None, B_HEADS, ts, CHUNK), lambda bi, i: (bi, 0, i, 0)),
                   pl.BlockSpec((None, nck, B_HEADS, LANE, CHUNK), lambda bi, i: (bi, i, 0, 0, 0)),
                   pl.BlockSpec((None, nck, B_HEADS, LANE), lambda bi, i: (bi, i, 0, 0))],
        out_shape=[jax.ShapeDtypeStruct((b, s, w), F32), jax.ShapeDtypeStruct((b, s, w), BF),
                   jax.ShapeDtypeStruct((b, s, w), BF),
                   jax.ShapeDtypeStruct((b, B_HEADS, s, CHUNK), BF),
                   jax.ShapeDtypeStruct((b, s // CHUNK, B_HEADS, LANE, CHUNK), BF),
                   jax.ShapeDtypeStruct((b, s // CHUNK, B_HEADS, LANE), F32)],
        compiler_params=_params("parallel", "parallel"),
        name="gdn_wy",
    )(q, k, v, colp, rowp)


def _gdn_scan_body(u_ref, w_ref, qg_ref, attn_ref, kdt_ref, dec_ref, gate_ref, ong_ref,
                   y_ref, s_ref, *, nck, bb):
    L = CHUNK

    @pl.when(pl.program_id(1) == 0)
    def _():
        s_ref[...] = jnp.zeros_like(s_ref)

    ch = [(bi, h, slice(h * LANE, (h + 1) * LANE)) for bi in range(bb) for h in range(B_HEADS)]
    for ci in range(nck):
        rows = slice(ci * L, (ci + 1) * L)
        sb = [s_ref[bi, h].astype(BF) for bi, h, _ in ch]
        ws = [_dot(w_ref[bi, rows, hs], sb[i]) for i, (bi, _, hs) in enumerate(ch)]
        qs = [_dot(qg_ref[bi, rows, hs], sb[i]) for i, (bi, _, hs) in enumerate(ch)]
        vb = [(u_ref[bi, rows, hs] - ws[i]).astype(BF) for i, (bi, _, hs) in enumerate(ch)]
        upd = [_dot(kdt_ref[bi, ci, h], vb[i]) for i, (bi, h, _) in enumerate(ch)]
        av = [_dot(attn_ref[bi, h, rows, :], vb[i]) for i, (bi, h, _) in enumerate(ch)]
        for i, (bi, h, hs) in enumerate(ch):
            s_ref[bi, h] = s_ref[bi, h] * dec_ref[bi, ci, h:h + 1, :] + upd[i]
        for i, (bi, h, hs) in enumerate(ch):
            on = _rms(qs[i] + av[i], ong_ref[...])
            y_ref[bi, rows, hs] = (on * _silu(gate_ref[bi, rows, hs].astype(F32))).astype(BF)


def _gdn_scan(u, w, qg, attn, kdt, dec, gate, onorm_g, *, ts=256, bb=8):
    b, s, wd = u.shape
    bb = min(bb, b)
    nck = ts // CHUNK
    tok = lambda n: pl.BlockSpec((bb, ts, n), lambda bi, i: (bi, i, 0))
    return pl.pallas_call(
        functools.partial(_gdn_scan_body, nck=nck, bb=bb),
        grid=(b // bb, s // ts),
        in_specs=[tok(wd), tok(wd), tok(wd),
                  pl.BlockSpec((bb, B_HEADS, ts, CHUNK), lambda bi, i: (bi, 0, i, 0)),
                  pl.BlockSpec((bb, nck, B_HEADS, LANE, CHUNK), lambda bi, i: (bi, i, 0, 0, 0)),
                  pl.BlockSpec((bb, nck, B_HEADS, LANE), lambda bi, i: (bi, i, 0, 0)),
                  tok(wd), pl.BlockSpec((1, LANE), lambda bi, i: (0, 0))],
        out_specs=tok(wd),
        out_shape=jax.ShapeDtypeStruct((b, s, wd), BF),
        scratch_shapes=[pltpu.VMEM((bb, B_HEADS, LANE, LANE), F32)],
        compiler_params=_params("parallel", "arbitrary"),
        name="gdn_scan",
    )(u, w, qg, attn, kdt, dec, gate, onorm_g.reshape(1, LANE))


def _head_lane(h):
    return h // 2 + (C_HEADS // 2) * (h % 2)


def _odd_pre_body(x_ref, gpre_ref, wm_ref, wst_ref, bc_ref,
                  q_ref, k_ref, v_ref, og_ref, colp_ref, rowp_ref, *, ts):
    nqk = C_HEADS * C_DQK
    nv = C_HEADS * LANE
    xn = _rms(x_ref[...], gpre_ref[...]).astype(BF)
    zr = _dot_nt(wst_ref[...], xn)
    q_ref[...] = _dot(xn, wm_ref[:, 0:nqk]).astype(BF)
    k_ref[...] = (_dot(xn, wm_ref[:, nqk:2 * nqk]) * (C_DQK ** -0.5)).astype(BF)
    v_ref[...] = _dot(xn, wm_ref[:, 2 * nqk:2 * nqk + nv]).astype(BF)

    capr = GATE_CAP * jnp.tanh((zr + bc_ref[...]) * (1.0 / GATE_CAP))
    bcr = _chunk_cumsum_rows(-_softplus(-capr))
    og_ref[...] = _dot(xn, wm_ref[:, 2 * nqk + nv:]).astype(BF)
    sub = lax.broadcasted_iota(jnp.int32, (2 * C_HEADS, ts), 0)
    padded = jnp.concatenate([jnp.where(sub < C_HEADS, capr, bcr), jnp.zeros((LANE - 2 * C_HEADS, ts), F32)], axis=0)
    for t in range(0, ts, LANE):
        colp_ref[t:t + LANE, :] = padded[:, t:t + LANE].T
    brow = capr[0:C_HEADS] - bcr[C_HEADS:2 * C_HEADS]
    right = pltpu.roll(brow, CHUNK, axis=1)
    left = pltpu.roll(brow, ts - CHUNK, axis=1)
    npair = C_HEADS // 2
    lo = lax.broadcasted_iota(jnp.int32, (npair, LANE), 1) < CHUNK
    for ci in range(ts // CHUNK):
        tile = slice((ci // 2) * LANE, (ci // 2 + 1) * LANE)
        if ci % 2 == 0:
            rowp_ref[ci] = jnp.where(lo, brow[0:npair, tile], right[npair:C_HEADS, tile])
        else:
            rowp_ref[ci] = jnp.where(lo, left[0:npair, tile], brow[npair:C_HEADS, tile])


def _odd_pre(x3, g_pre, w_in, b_i, b_f, *, ts=512):
    b, s, d = x3.shape
    nqk = C_HEADS * C_DQK
    nv = C_HEADS * LANE
    o_g = 2 * nqk + nv
    wm = jnp.concatenate([w_in[:, :o_g], w_in[:, o_g + 2 * C_HEADS:]], axis=1).astype(BF)
    wsm = w_in[:, o_g:o_g + 2 * C_HEADS]
    bias = jnp.concatenate([b_i, b_f])
    order = jnp.array([g * C_HEADS + h for g in range(2) for h in (*range(0, C_HEADS, 2), *range(1, C_HEADS, 2))])
    wst = wsm.T[order].astype(BF)
    tok = lambda n: pl.BlockSpec((None, ts, n), lambda bi, i: (bi, i, 0))
    vec = lambda n: pl.BlockSpec((1, n), lambda bi, i: (0, 0))
    nck = ts // CHUNK
    return pl.pallas_call(
        functools.partial(_odd_pre_body, ts=ts),
        grid=(b, s // ts),
        in_specs=[tok(d), vec(d), _vmem(), _vmem(), _vmem()],
        out_specs=[tok(nqk), tok(nqk), tok(nv), tok(nv), tok(LANE),
                   pl.BlockSpec((None, nck, C_HEADS // 2, LANE), lambda bi, i: (bi, i, 0, 0))],
        out_shape=[jax.ShapeDtypeStruct((b, s, nqk), BF), jax.ShapeDtypeStruct((b, s, nqk), BF),
                   jax.ShapeDtypeStruct((b, s, nv), BF), jax.ShapeDtypeStruct((b, s, nv), BF),
                   jax.ShapeDtypeStruct((b, s, LANE), F32),
                   jax.ShapeDtypeStruct((b, s // CHUNK, C_HEADS // 2, LANE), F32)],
        compiler_params=_params("parallel", "parallel"),
        name="odd_pre",
    )(x3, g_pre.reshape(1, d), wm, wst, bias[order].reshape(2 * C_HEADS, 1))


def _mlstm_local_body(q_ref, k_ref, colp_ref, rowp_ref,
                      pk_ref, qs_ref, en_ref, se_ref, m_ref, *, nck):
    L = CHUNK
    npair = C_HEADS // 2
    row = lax.broadcasted_iota(jnp.int32, (L, LANE), 0)
    lane = lax.broadcasted_iota(jnp.int32, (L, LANE), 1)
    lo = lane < C_DQK
    causal = row >= (lane & (L - 1))
    spread = [jnp.where(lo, _head_lane(2 * p), _head_lane(2 * p + 1)) for p in range(npair)]
    zero = jnp.zeros((L, LANE), BF)

    @pl.when(pl.program_id(1) == 0)
    def _():
        m_ref[...] = jnp.zeros_like(m_ref)

    m = m_ref[0:1, :]
    g, m_in, wsc = [], [], []
    for ci in range(nck):
        rows = slice(ci * L, (ci + 1) * L)
        colp = colp_ref[rows, :]
        li = jnp.where(lane < C_HEADS, colp, 0.0)
        bc = jnp.where(lane < C_HEADS, pltpu.roll(colp, LANE - C_HEADS, axis=1), 0.0)
        cm = li - bc
        for s in (1, 2, 4, 8, 16, 32):
            cm = jnp.maximum(cm, jnp.where(row >= s, pltpu.roll(cm, s, axis=0), -jnp.inf))
        gi = jnp.maximum(m, cm)
        en_ref[rows, :] = jnp.exp(-bc - gi)
        bl = bc[L - 1:L, :]
        we = bl - bc + li
        m_new = jnp.maximum(bl + m, jnp.max(we, axis=0, keepdims=True))
        sev = jnp.exp(bl + m - m_new)
        for h in range(C_HEADS):
            se_ref[ci, h:h + 1, :] = jnp.broadcast_to(sev[:, _head_lane(h):_head_lane(h) + 1], (1, LANE))
        g.append(gi)
        m_in.append(jnp.broadcast_to(m, (8, LANE)))
        wsc.append(we - m_new)
        m = m_new
    m_ref[...] = jnp.broadcast_to(m, m_ref.shape)

    units = [(ci, p) for ci in range(nck) for p in range(npair)]
    gp = [jnp.take_along_axis(g[ci], spread[p], axis=1) for ci, p in units]
    mp = [jnp.take_along_axis(m_in[ci], spread[p][0:8], axis=1)[0:1] for ci, p in units]
    wp = [jnp.take_along_axis(wsc[ci], spread[p], axis=1) for ci, p in units]

    qk = []
    for ci, p in units:
        rows, ps = slice(ci * L, (ci + 1) * L), slice(p * LANE, (p + 1) * LANE)
        kp = k_ref[rows, ps]
        qk.append(_dot_nt(q_ref[rows, ps], jnp.concatenate([jnp.where(lo, kp, zero), jnp.where(lo, zero, kp)], axis=0)))
    for i, (ci, p) in enumerate(units):
        rows, ps = slice(ci * L, (ci + 1) * L), slice(p * LANE, (p + 1) * LANE)
        pm = (jnp.exp(jnp.where(causal, rowp_ref[ci, p:p + 1, :] - gp[i], -jnp.inf)) * qk[i]).astype(BF)
        pk_ref[ci, p, 0:L, :] = jnp.where(lo, pm, zero)
        pk_ref[ci, p, L:2 * L, :] = jnp.where(lo, zero, pm)
        qs = (q_ref[rows, ps].astype(F32) * jnp.exp(mp[i] - gp[i])).astype(BF)
        qs_ref[ci, p, 0:L, :] = jnp.where(lo, qs, zero)
        qs_ref[ci, p, L:2 * L, :] = jnp.where(lo, zero, qs)
    for i, (ci, p) in enumerate(units):
        rows, ps = slice(ci * L, (ci + 1) * L), slice(p * LANE, (p + 1) * LANE)
        kw = k_ref[rows, ps].astype(F32) * jnp.exp(wp[i])
        kw2 = jnp.concatenate([jnp.where(lo, kw, 0.0), jnp.where(lo, 0.0, kw)], axis=0)
        pk_ref[ci, p, 2 * L:4 * L, :] = kw2.T.astype(BF)


def _mlstm_local(q, k, colp, rowp, *, ts=512):
    b, s, nqk = q.shape
    nck = ts // CHUNK
    npair = C_HEADS // 2
    tok = lambda n: pl.BlockSpec((None, ts, n), lambda bi, i: (bi, i, 0))
    return pl.pallas_call(
        functools.partial(_mlstm_local_body, nck=nck),
        grid=(b, s // ts),
        in_specs=[tok(nqk), tok(nqk), tok(LANE),
                  pl.BlockSpec((None, nck, npair, LANE), lambda bi, i: (bi, i, 0, 0))],
        out_specs=[pl.BlockSpec((None, nck, npair, 4 * CHUNK, LANE), lambda bi, i: (bi, i, 0, 0, 0)),
                   pl.BlockSpec((None, nck, npair, 2 * CHUNK, LANE), lambda bi, i: (bi, i, 0, 0, 0)),
                   tok(LANE),
                   pl.BlockSpec((None, nck, C_HEADS, LANE), lambda bi, i: (bi, i, 0, 0))],
        out_shape=[jax.ShapeDtypeStruct((b, s // CHUNK, npair, 4 * CHUNK, LANE), BF),
                   jax.ShapeDtypeStruct((b, s // CHUNK, npair, 2 * CHUNK, LANE), BF),
                   jax.ShapeDtypeStruct((b, s, LANE), F32),
                   jax.ShapeDtypeStruct((b, s // CHUNK, C_HEADS, LANE), F32)],
        scratch_shapes=[pltpu.VMEM((C_HEADS, LANE), F32)],
        compiler_params=_params("parallel", "arbitrary"),
        name="mlstm_local",
    )(q, k, colp, rowp)


def _mlstm_scan_body(pk_ref, qs_ref, en_ref, se_ref, v_ref, og_ref, ong_ref,
                     y_ref, c_ref, *, nck, bb):
    L = CHUNK
    ones = jnp.ones((L, LANE), BF)

    @pl.when(pl.program_id(1) == 0)
    def _():
        c_ref[...] = jnp.zeros_like(c_ref)

    ch = [(bi, h) for bi in range(bb) for h in range(C_HEADS)]
    pairs = [(bi, p) for bi in range(bb) for p in range(C_HEADS // 2)]
    hsl = lambda h: slice(h * LANE, (h + 1) * LANE)
    srl = lambda h: slice((h % 2) * C_DQK, (h % 2 + 1) * C_DQK)
    for ci in range(nck):
        rows = slice(ci * L, (ci + 1) * L)
        qc = {(bi, p): _dot(qs_ref[bi, ci, p], c_ref[bi, p].astype(BF)) for bi, p in pairs}
        pv = {}
        for bi, p in pairs:
            vst = jnp.concatenate(
                [jnp.concatenate([v_ref[bi, rows, hsl(2 * p + sub)], ones], axis=1) for sub in range(2)], axis=0)
            pv[(bi, p)] = _dot(pk_ref[bi, ci, p], vst)
        for bi, h in ch:
            se = se_ref[bi, ci, h:h + 1, :]
            c_ref[bi, h // 2, srl(h), :] = (c_ref[bi, h // 2, srl(h), :] * jnp.concatenate([se, se], axis=1)
                                            + pv[(bi, h // 2)][2 * L + (h % 2) * L:3 * L + (h % 2) * L])
        res = [qc[(bi, h // 2)][srl(h)] + pv[(bi, h // 2)][srl(h)] for bi, h in ch]
        den = [jnp.maximum(jnp.abs(res[i][:, LANE:]), en_ref[bi, rows, _head_lane(h):_head_lane(h) + 1])
               for i, (bi, h) in enumerate(ch)]
        hh = [_sigmoid(og_ref[bi, rows, hsl(h)].astype(F32)) * (res[i][:, :LANE] / den[i])
              for i, (bi, h) in enumerate(ch)]
        for i, (bi, h) in enumerate(ch):
            y_ref[bi, rows, hsl(h)] = _rms(hh[i], ong_ref[:, hsl(h)]).astype(BF)


def _mlstm_scan(pk, qs, en, se, v, og, onorm_g, *, ts=256, bb=4):
    b, s, nv = v.shape
    bb = min(bb, b)
    nck = ts // CHUNK
    npair = C_HEADS // 2
    tok = lambda n: pl.BlockSpec((bb, ts, n), lambda bi, i: (bi, i, 0))
    return pl.pallas_call(
        functools.partial(_mlstm_scan_body, nck=nck, bb=bb),
        grid=(b // bb, s // ts),
        in_specs=[pl.BlockSpec((bb, nck, npair, 4 * CHUNK, LANE), lambda bi, i: (bi, i, 0, 0, 0)),
                  pl.BlockSpec((bb, nck, npair, 2 * CHUNK, LANE), lambda bi, i: (bi, i, 0, 0, 0)),
                  tok(LANE),
                  pl.BlockSpec((bb, nck, C_HEADS, LANE), lambda bi, i: (bi, i, 0, 0)),
                  tok(nv), tok(nv), pl.BlockSpec((1, nv), lambda bi, i: (0, 0))],
        out_specs=tok(nv),
        out_shape=jax.ShapeDtypeStruct((b, s, nv), BF),
        scratch_shapes=[pltpu.VMEM((bb, npair, LANE, 2 * LANE), F32)],
        compiler_params=_params("parallel", "arbitrary"),
        name="mlstm_scan",
    )(pk, qs, en, se, v, og, onorm_g.reshape(1, nv))


def kernel(x, norm_g, ffn1_gu, ffn1_down, ffn2_gu, ffn2_down, ev_w_in, ev_w_out, ev_a_ln_g, ev_a_ws, ev_a_bs, ev_b_conv, ev_b_a_log, ev_b_dt_bias, ev_b_onorm_g, od_w_in, od_w_out, od_b_i, od_b_f, od_onorm_g):
    b, s, d = x.shape
    x2 = x.reshape(b * s, d)
    depth = norm_g.shape[0]
    for layer in range(depth):
        ng = norm_g[layer]
        j = layer // 2
        x2 = _ffn(x2, ng[0], _to_bf16(ffn1_gu, layer), _to_bf16(ffn1_down, layer), ng[1])
        x3 = x2.reshape(b, s, d)
        if layer % 2 == 0:
            ya, q, k, v, gate, colp, rowp = _even_pre(
                x3, ng[2], ev_w_in[j], ev_a_ln_g[j], ev_a_ws[j], ev_a_bs[j], ev_b_conv[j],
                ev_b_a_log[j], ev_b_dt_bias[j])
            u, w, qg, attn, kdt, dec = _gdn_wy(q, k, v, colp, rowp)
            yb = _gdn_scan(u, w, qg, attn, kdt, dec, gate, ev_b_onorm_g[j])
            mix = (ya.reshape(b * s, -1), 0, yb.reshape(b * s, -1), 0, ev_w_out[j])
        else:
            q, k, v, og, colp, rowp = _odd_pre(x3, ng[2], od_w_in[j], od_b_i[j], od_b_f[j])
            pk, qs, en, se = _mlstm_local(q, k, colp, rowp)
            y = _mlstm_scan(pk, qs, en, se, v, og, od_onorm_g[j]).reshape(b * s, -1)
            mix = (y, 0, y, 1, od_w_out[j])
        x2 = _post_ffn(*mix, ng[3], x2, ng[4], _to_bf16(ffn2_gu, layer), _to_bf16(ffn2_down, layer), ng[5])
    return x2.reshape(b, s, d)
```

```python
import functools

import jax
import jax.numpy as jnp
from jax import lax
from jax.experimental import pallas as pl
from jax.experimental.pallas import tpu as pltpu

F32 = jnp.float32
BF = jnp.bfloat16
HI = lax.Precision.HIGHEST

EPS = 1e-6
LANE = 128
CHUNK = 64
A_GROUPS, A_CHUNK = 4, 128
B_HEADS, B_CONV = 4, 4
C_HEADS, C_DQK = 8, 64
GATE_CAP = 15.0
VMEM_LIMIT = 56 * 1024 * 1024


def _dot(a, b, prec=None):
    return jnp.dot(a, b, preferred_element_type=F32, precision=prec)


def _dot_nt(a, b, prec=None):
    return lax.dot_general(a, b, (((1,), (1,)), ((), ())), preferred_element_type=F32, precision=prec)


def _rms(xf, g):
    return xf * lax.rsqrt(jnp.mean(xf * xf, axis=-1, keepdims=True) + EPS) * g


def _sigmoid(x):
    return 1.0 / (1.0 + jnp.exp(-x))


def _silu(x):
    return x * _sigmoid(x)


def _gelu(x):
    return 0.5 * x * (1.0 + jnp.tanh(0.7978845608028654 * (x + 0.044715 * (x * x * x))))


def _softplus(x):
    return jnp.maximum(x, 0.0) + jnp.log1p(jnp.exp(-jnp.abs(x)))


def _chunk_tri(transpose):
    r = lax.broadcasted_iota(jnp.int32, (LANE, LANE), 1 if transpose else 0)
    c = lax.broadcasted_iota(jnp.int32, (LANE, LANE), 0 if transpose else 1)
    same = lax.shift_right_logical(r, 6) == lax.shift_right_logical(c, 6)
    return jnp.where(same, jnp.where(c <= r, 1.0, 0.0), 0.0).astype(BF)


def _split3(x):
    p1 = x.astype(BF).astype(F32)
    r1 = x - p1
    p2 = r1.astype(BF).astype(F32)
    return p1, p2, (r1 - p2).astype(BF).astype(F32)


SPLIT_LANES = 16


def _chunk_cumsum_cols(x):
    lane = lax.broadcasted_iota(jnp.int32, x.shape, 1)
    p1, p2, p3 = _split3(jnp.where(lane < SPLIT_LANES, x, 0.0))
    packed = (p1 + pltpu.roll(p2, SPLIT_LANES, axis=1) + pltpu.roll(p3, 2 * SPLIT_LANES, axis=1)).astype(BF)
    tri = _chunk_tri(False)
    res = jnp.concatenate([_dot(tri, packed[t:t + LANE]) for t in range(0, x.shape[0], LANE)], axis=0)
    return res + pltpu.roll(res, LANE - SPLIT_LANES, axis=1) + pltpu.roll(res, LANE - 2 * SPLIT_LANES, axis=1)


def _chunk_cumsum_rows(x):
    r = x.shape[0]
    p1, p2, p3 = _split3(x)
    stacked = jnp.concatenate([p1, p2, p3, jnp.zeros_like(p1)], axis=0).astype(BF)
    tri_t = _chunk_tri(True)
    res = jnp.concatenate([_dot(stacked[:, t:t + LANE], tri_t) for t in range(0, x.shape[1], LANE)], axis=1)
    return res[0:r] + res[r:2 * r] + res[2 * r:3 * r]


def _params(*sem):
    return pltpu.CompilerParams(dimension_semantics=sem, vmem_limit_bytes=VMEM_LIMIT)


def _vmem():
    return pl.BlockSpec(memory_space=pltpu.VMEM)


def _cast_body(w_ref, o_ref):
    o_ref[...] = w_ref[...].astype(BF)


def _to_bf16(w_stack, layer, *, steps=8):
    _, rows, cols = w_stack.shape
    tr = rows // steps
    return pl.pallas_call(
        _cast_body,
        grid=(steps,),
        in_specs=[pl.BlockSpec((None, tr, cols), lambda i: (layer, i, 0))],
        out_specs=pl.BlockSpec((tr, cols), lambda i: (i, 0)),
        out_shape=jax.ShapeDtypeStruct((rows, cols), BF),
        compiler_params=_params("parallel"),
        name="cast_bf16",
    )(w_stack)


def _swiglu_step(x, gpre_ref, wgu_ref, wd_ref, gpost_ref, o_ref, acc_ref, tf):
    xn = _rms(x, gpre_ref[...]).astype(BF)
    f = wd_ref.shape[0]

    def gate_up(j):
        return (_dot(xn, wgu_ref[:, j * tf:(j + 1) * tf]), _dot(xn, wgu_ref[:, f + j * tf:f + (j + 1) * tf]))

    n_chunks = f // tf
    gu = gate_up(0)
    for j in range(n_chunks):
        g, u = gu
        if j + 1 < n_chunks:
            gu = gate_up(j + 1)
        d = _dot((_silu(g) * u).astype(BF), wd_ref[j * tf:(j + 1) * tf, :])
        if j == 0:
            acc_ref[...] = d
        else:
            acc_ref[...] += d
    o_ref[...] = x + _rms(acc_ref[...], 0.5 * gpost_ref[...])


def _ffn_body(x_ref, gpre_ref, wgu_ref, wd_ref, gpost_ref, o_ref, acc_ref, *, tf):
    _swiglu_step(x_ref[...], gpre_ref, wgu_ref, wd_ref, gpost_ref, o_ref, acc_ref, tf)


def _ffn(x2, g_pre, w_gu, w_down, g_post, *, tm=512, tf=256):
    m, d = x2.shape
    row = pl.BlockSpec((tm, d), lambda i: (i, 0))
    vec = pl.BlockSpec((1, d), lambda i: (0, 0))
    return pl.pallas_call(
        functools.partial(_ffn_body, tf=tf),
        grid=(m // tm,),
        in_specs=[row, vec, _vmem(), _vmem(), vec],
        out_specs=row,
        out_shape=jax.ShapeDtypeStruct((m, d), F32),
        scratch_shapes=[pltpu.VMEM((tm, d), F32)],
        compiler_params=_params("parallel"),
        name="ffn",
    )(x2, g_pre.reshape(1, d), w_gu, w_down, g_post.reshape(1, d))


def _post_ffn_body(y1_ref, y2_ref, w1_ref, w2_ref, gmix_ref, x_ref, gpre_ref, wgu_ref, wd_ref, gpost_ref,
                   o_ref, acc_ref, *, tf):
    y = _dot(y1_ref[...], w1_ref[...]) + _dot(y2_ref[...], w2_ref[...])
    x = x_ref[...] + _rms(y, gmix_ref[...])
    _swiglu_step(x, gpre_ref, wgu_ref, wd_ref, gpost_ref, o_ref, acc_ref, tf)


def _post_ffn(y1, c1, y2, c2, w_out, g_mix, x2, g_pre, w_gu, w_down, g_post, *, tm=512, tf=256):
    m, d = x2.shape
    half = w_out.shape[0] // 2
    w = w_out.astype(BF)
    row = pl.BlockSpec((tm, d), lambda i: (i, 0))
    vec = pl.BlockSpec((1, d), lambda i: (0, 0))
    return pl.pallas_call(
        functools.partial(_post_ffn_body, tf=tf),
        grid=(m // tm,),
        in_specs=[pl.BlockSpec((tm, half), lambda i: (i, c1)), pl.BlockSpec((tm, half), lambda i: (i, c2)),
                  _vmem(), _vmem(), vec, row, vec, _vmem(), _vmem(), vec],
        out_specs=row,
        out_shape=jax.ShapeDtypeStruct((m, d), F32),
        scratch_shapes=[pltpu.VMEM((tm, d), F32)],
        compiler_params=_params("parallel"),
        name="post_ffn",
    )(y1, y2, w[:half], w[half:], g_mix.reshape(1, d), x2, g_pre.reshape(1, d),
      w_gu, w_down, g_post.reshape(1, d))


def _even_pre_body(x_ref, gpre_ref, wm_ref, wst_ref, lng_ref, aws_ref, bst_ref, cw_ref,
                   alog_c_ref, dtb_c_ref,
                   y_ref, q_ref, k_ref, v_ref, gate_ref, colp_ref, rowp_ref, zbuf_ref, *, ts):
    aw = A_GROUPS * LANE
    cq = 3 * B_HEADS * LANE
    xn = _rms(x_ref[...], gpre_ref[...]).astype(BF)

    zr = _dot_nt(wst_ref[...], xn)

    @pl.when(pl.program_id(1) == 0)
    def _():
        zbuf_ref[0:8, :] = jnp.zeros((8, cq), F32)

    zbuf_ref[8:8 + ts, :] = _dot(xn, wm_ref[:, 2 * aw:2 * aw + cq])
    zu = _dot(xn, wm_ref[:, 0:aw])
    zv = _dot(xn, wm_ref[:, aw:2 * aw])

    grow = -jnp.exp(alog_c_ref[...]) * _softplus(zr + dtb_c_ref[...])
    sub = lax.broadcasted_iota(jnp.int32, (8, ts), 0)
    gates = jnp.where(sub < B_HEADS, _sigmoid(zr), _chunk_cumsum_rows(grow))
    for ci in range(ts // CHUNK):
        rowp_ref[ci] = gates[:, ci * CHUNK:(ci + 1) * CHUNK]
    padded = jnp.concatenate([gates, jnp.zeros((LANE - 8, ts), F32)], axis=0)
    for t in range(0, ts, LANE):
        colp_ref[t:t + LANE, :] = padded[:, t:t + LANE].T
    gate_ref[...] = _dot(xn, wm_ref[:, 2 * aw + cq:]).astype(BF)

    u = _gelu(zu)
    v = _gelu(zv)
    r = lax.broadcasted_iota(jnp.int32, (A_CHUNK, A_CHUNK), 0)
    c = lax.broadcasted_iota(jnp.int32, (A_CHUNK, A_CHUNK), 1)
    for g in range(A_GROUPS):
        gs = slice(g * LANE, (g + 1) * LANE)
        vg = v[:, gs]
        d = vg - jnp.mean(vg, axis=-1, keepdims=True)
        var = jnp.mean(d * d, axis=-1, keepdims=True)
        vn = (d * lax.rsqrt(var + EPS) * lng_ref[:, gs]).astype(BF)
        w = jnp.where(r >= c, aws_ref[g], 0.0).astype(BF)
        bcol = bst_ref[:, g:g + 1]
        for ci in range(ts // A_CHUNK):
            rs = slice(ci * A_CHUNK, (ci + 1) * A_CHUNK)
            mixed = _dot(w, vn[rs]) + bcol
            y_ref[rs, gs] = (u[rs, gs] * mixed).astype(BF)

    for t in range(3 * B_HEADS):
        cs = slice(t * LANE, (t + 1) * LANE)
        zz = zbuf_ref[:, cs]
        sz = pltpu.roll(zz, 1, axis=0)
        near = cw_ref[3:4, cs] * zz + cw_ref[2:3, cs] * sz
        far = cw_ref[1:2, cs] * zz + cw_ref[0:1, cs] * sz
        a = _silu((near + pltpu.roll(far, 2, axis=0))[8:])
        if t < 2 * B_HEADS:
            a = a * lax.rsqrt(jnp.sum(a * a, axis=-1, keepdims=True) + EPS)
        if t < B_HEADS:
            q_ref[:, cs] = (a * (LANE ** -0.5)).astype(BF)
        elif t < 2 * B_HEADS:
            k_ref[:, slice((t - B_HEADS) * LANE, (t - B_HEADS + 1) * LANE)] = a.astype(BF)
        else:
            v_ref[:, slice((t - 2 * B_HEADS) * LANE, (t - 2 * B_HEADS + 1) * LANE)] = a.astype(BF)
    zbuf_ref[0:8, :] = zbuf_ref[ts:ts + 8, :]


def _even_pre(x3, g_pre, w_in, a_ln_g, a_ws, a_bs, b_conv, a_log, dt_bias, *, ts=1024):
    assert B_CONV == 4 and b_conv.shape[0] == B_CONV
    b, s, d = x3.shape
    aw = A_GROUPS * LANE
    nq = 3 * B_HEADS * LANE
    o_beta = 2 * aw + nq
    wm = jnp.concatenate([w_in[:, :o_beta], w_in[:, o_beta + 2 * B_HEADS:]], axis=1).astype(BF)
    wst = w_in[:, o_beta:o_beta + 2 * B_HEADS].T.astype(BF)
    sub_pad = lambda p: jnp.pad(p, (B_HEADS, 0)).reshape(2 * B_HEADS, 1)
    grid = (b, s // ts)
    tok = lambda w: pl.BlockSpec((None, ts, w), lambda bi, i: (bi, i, 0))
    vec = lambda w: pl.BlockSpec((1, w), lambda bi, i: (0, 0))
    nck = ts // CHUNK
    outs = pl.pallas_call(
        functools.partial(_even_pre_body, ts=ts),
        grid=grid,
        in_specs=[tok(d), vec(d), _vmem(), _vmem(), vec(aw), _vmem(), _vmem(), _vmem(), _vmem(), _vmem()],
        out_specs=[tok(aw), tok(aw), tok(aw), tok(aw), tok(aw), tok(LANE),
                   pl.BlockSpec((None, nck, 8, CHUNK), lambda bi, i: (bi, i, 0, 0))],
        out_shape=[jax.ShapeDtypeStruct((b, s, aw), BF)] * 5
        + [jax.ShapeDtypeStruct((b, s, LANE), F32), jax.ShapeDtypeStruct((b, s // CHUNK, 8, CHUNK), F32)],
        scratch_shapes=[pltpu.VMEM((ts + 8, nq), F32)],
        compiler_params=_params("parallel", "arbitrary"),
        name="even_pre",
    )(x3, g_pre.reshape(1, d), wm, wst, a_ln_g.reshape(1, aw), a_ws, a_bs.T, b_conv,
      sub_pad(a_log), sub_pad(dt_bias))
    return outs


def _gdn_wy_body(q_ref, k_ref, v_ref, colp_ref, rowp_ref,
                 u_ref, w_ref, qg_ref, attn_ref, kdt_ref, dec_ref, *, nck, cg):
    L = CHUNK
    r = lax.broadcasted_iota(jnp.int32, (L, L), 0)
    c = lax.broadcasted_iota(jnp.int32, (L, L), 1)
    causal = r >= c
    strict = r > c
    eye = jnp.where(r == c, 1.0, 0.0).astype(F32)
    shr = lax.shift_right_logical
    diag8 = jnp.where(shr(r, 3) == shr(c, 3), 1.0, 0.0).astype(F32)
    off = [jnp.where((shr(r, t + 1) == shr(c, t + 1)) & (shr(r, t) > shr(c, t)), 1.0, 0.0).astype(F32)
           for t in (3, 4, 5)]

    def mm(a, b):
        return _dot(a.astype(BF), b.astype(BF))

    def group(units):
        n = range(len(units))
        rows = [slice(ci * L, (ci + 1) * L) for ci, _ in units]
        hs = [slice(h * LANE, (h + 1) * LANE) for _, h in units]
        kh = [k_ref[rows[i], hs[i]] for i in n]
        khf = [kh[i].astype(F32) for i in n]
        beta = [colp_ref[rows[i], h:h + 1] for i, (_, h) in enumerate(units)]
        gc = [colp_ref[rows[i], B_HEADS + h:B_HEADS + h + 1] for i, (_, h) in enumerate(units)]
        decay = [jnp.exp(jnp.where(causal, gc[i] - rowp_ref[ci, B_HEADS + h:B_HEADS + h + 1, :], -jnp.inf))
                 for i, (ci, h) in enumerate(units)]
        kb = [khf[i] * beta[i] for i in n]
        kk = [_dot_nt(kb[i].astype(BF), kh[i]) for i in n]
        qk = [_dot_nt(q_ref[rows[i], hs[i]], kh[i]) for i in n]
        low = [jnp.where(strict, kk[i] * decay[i], 0.0) for i in n]
        for i, (_, h) in enumerate(units):
            attn_ref[h, rows[i], :] = jnp.where(causal, qk[i] * decay[i], 0.0).astype(BF)
        pw = [-(low[i] * diag8) for i in n]
        inv = [eye + pw[i] for i in n]
        for _ in range(2):
            pw = [mm(pw[i], pw[i]) for i in n]
            t = [mm(inv[i], pw[i]) for i in n]
            inv = [inv[i] + t[i] for i in n]
        for m in off:
            t = [mm(low[i] * m, inv[i]) for i in n]
            t = [mm(inv[i], t[i]) for i in n]
            inv = [inv[i] - t[i] for i in n]
        eg = [jnp.exp(gc[i]) for i in n]
        sol = [mm(inv[i], jnp.concatenate([v_ref[rows[i], hs[i]].astype(F32) * beta[i], kb[i] * eg[i]], axis=1))
               for i in n]
        for i, (ci, h) in enumerate(units):
            u_ref[rows[i], hs[i]] = sol[i][:, :LANE]
            w_ref[rows[i], hs[i]] = sol[i][:, LANE:].astype(BF)
            qg_ref[rows[i], hs[i]] = (q_ref[rows[i], hs[i]].astype(F32) * eg[i]).astype(BF)
            gl = gc[i][L - 1:L, :]
            kd = khf[i] * jnp.exp(gl - gc[i])
            kdt = jnp.concatenate([kd, jnp.zeros_like(kd)], axis=0).T
            kdt_ref[ci, h] = kdt[:, :L].astype(BF)
            dec_ref[ci, h:h + 1, :] = jnp.broadcast_to(jnp.exp(gl), (1, LANE))

    for c0 in range(0, nck, cg):
        group([(ci, h) for ci in range(c0, c0 + cg) for h in range(B_HEADS)])


def _gdn_wy(q, k, v, colp, rowp, *, ts=512, cg=8):
    b, s, w = q.shape
    nck = ts // CHUNK
    grid = (b, s // ts)
    tok = lambda n: pl.BlockSpec((None, ts, n), lambda bi, i: (bi, i, 0))
    return pl.pallas_call(
        functools.partial(_gdn_wy_body, nck=nck, cg=cg),
        grid=grid,
        in_specs=[tok(w), tok(w), tok(w), tok(LANE),
                  pl.BlockSpec((None, nck, 8, CHUNK), lambda bi, i: (bi, i, 0, 0))],
        out_specs=[tok(w), tok(w), tok(w),
                   pl.BlockSpec((None, B_HEADS, ts, CHUNK), lambda bi, i: (bi, 0, i, 0)),
                   pl.BlockSpec((None, nck, B_HEADS, LANE, CHUNK), lambda bi, i: (bi, i, 0, 0, 0)),
                   pl.BlockSpec((None, nck, B_HEADS, LANE), lambda bi, i: (bi, i, 0, 0))],
        out_shape=[jax.ShapeDtypeStruct((b, s, w), F32), jax.ShapeDtypeStruct((b, s, w), BF),
                   jax.ShapeDtypeStruct((b, s, w), BF),
                   jax.ShapeDtypeStruct((b, B_HEADS, s, CHUNK), BF),
                   jax.ShapeDtypeStruct((b, s // CHUNK, B_HEADS, LANE, CHUNK), BF),
                   jax.ShapeDtypeStruct((b, s // CHUNK, B_HEADS, LANE), F32)],
        compiler_params=_params("parallel", "parallel"),
        name="gdn_wy",
    )(q, k, v, colp, rowp)


def _gdn_scan_body(u_ref, w_ref, qg_ref, attn_ref, kdt_ref, dec_ref, gate_ref, ong_ref,
                   y_ref, s_ref, *, nck, bb):
    L = CHUNK

    @pl.when(pl.program_id(1) == 0)
    def _():
        s_ref[...] = jnp.zeros_like(s_ref)

    ch = [(bi, h, slice(h * LANE, (h + 1) * LANE)) for bi in range(bb) for h in range(B_HEADS)]
    for ci in range(nck):
        rows = slice(ci * L, (ci + 1) * L)
        sb = [s_ref[bi, h].astype(BF) for bi, h, _ in ch]
        ws = [_dot(w_ref[bi, rows, hs], sb[i]) for i, (bi, _, hs) in enumerate(ch)]
        qs = [_dot(qg_ref[bi, rows, hs], sb[i]) for i, (bi, _, hs) in enumerate(ch)]
        vb = [(u_ref[bi, rows, hs] - ws[i]).astype(BF) for i, (bi, _, hs) in enumerate(ch)]
        upd = [_dot(kdt_ref[bi, ci, h], vb[i]) for i, (bi, h, _) in enumerate(ch)]
        av = [_dot(attn_ref[bi, h, rows, :], vb[i]) for i, (bi, h, _) in enumerate(ch)]
        for i, (bi, h, hs) in enumerate(ch):
            s_ref[bi, h] = s_ref[bi, h] * dec_ref[bi, ci, h:h + 1, :] + upd[i]
        for i, (bi, h, hs) in enumerate(ch):
            on = _rms(qs[i] + av[i], ong_ref[...])
            y_ref[bi, rows, hs] = (on * _silu(gate_ref[bi, rows, hs].astype(F32))).astype(BF)


def _gdn_scan(u, w, qg, attn, kdt, dec, gate, onorm_g, *, ts=256, bb=8):
    b, s, wd = u.shape
    bb = min(bb, b)
    nck = ts // CHUNK
    tok = lambda n: pl.BlockSpec((bb, ts, n), lambda bi, i: (bi, i, 0))
    return pl.pallas_call(
        functools.partial(_gdn_scan_body, nck=nck, bb=bb),
        grid=(b // bb, s // ts),
        in_specs=[tok(wd), tok(wd), tok(wd),
                  pl.BlockSpec((bb, B_HEADS, ts, CHUNK), lambda bi, i: (bi, 0, i, 0)),
                  pl.BlockSpec((bb, nck, B_HEADS, LANE, CHUNK), lambda bi, i: (bi, i, 0, 0, 0)),
                  pl.BlockSpec((bb, nck, B_HEADS, LANE), lambda bi, i: (bi, i, 0, 0)),
                  tok(wd), pl.BlockSpec((1, LANE), lambda bi, i: (0, 0))],
        out_specs=tok(wd),
        out_shape=jax.ShapeDtypeStruct((b, s, wd), BF),
        scratch_shapes=[pltpu.VMEM((bb, B_HEADS, LANE, LANE), F32)],
        compiler_params=_params("parallel", "arbitrary"),
        name="gdn_scan",
    )(u, w, qg, attn, kdt, dec, gate, onorm_g.reshape(1, LANE))


def _head_lane(h):
    return h // 2 + (C_HEADS // 2) * (h % 2)


def _odd_pre_body(x_ref, gpre_ref, wm_ref, wst_ref, bc_ref,
                  q_ref, k_ref, v_ref, og_ref, colp_ref, rowp_ref, *, ts):
    nqk = C_HEADS * C_DQK
    nv = C_HEADS * LANE
    xn = _rms(x_ref[...], gpre_ref[...]).astype(BF)
    zr = _dot_nt(wst_ref[...], xn)
    q_ref[...] = _dot(xn, wm_ref[:, 0:nqk]).astype(BF)
    k_ref[...] = (_dot(xn, wm_ref[:, nqk:2 * nqk]) * (C_DQK ** -0.5)).astype(BF)
    v_ref[...] = _dot(xn, wm_ref[:, 2 * nqk:2 * nqk + nv]).astype(BF)

    capr = GATE_CAP * jnp.tanh((zr + bc_ref[...]) * (1.0 / GATE_CAP))
    bcr = _chunk_cumsum_rows(-_softplus(-capr))
    og_ref[...] = _dot(xn, wm_ref[:, 2 * nqk + nv:]).astype(BF)
    sub = lax.broadcasted_iota(jnp.int32, (2 * C_HEADS, ts), 0)
    padded = jnp.concatenate([jnp.where(sub < C_HEADS, capr, bcr), jnp.zeros((LANE - 2 * C_HEADS, ts), F32)], axis=0)
    for t in range(0, ts, LANE):
        colp_ref[t:t + LANE, :] = padded[:, t:t + LANE].T
    brow = capr[0:C_HEADS] - bcr[C_HEADS:2 * C_HEADS]
    right = pltpu.roll(brow, CHUNK, axis=1)
    left = pltpu.roll(brow, ts - CHUNK, axis=1)
    npair = C_HEADS // 2
    lo = lax.broadcasted_iota(jnp.int32, (npair, LANE), 1) < CHUNK
    for ci in range(ts // CHUNK):
        tile = slice((ci // 2) * LANE, (ci // 2 + 1) * LANE)
        if ci % 2 == 0:
            rowp_ref[ci] = jnp.where(lo, brow[0:npair, tile], right[npair:C_HEADS, tile])
        else:
            rowp_ref[ci] = jnp.where(lo, left[0:npair, tile], brow[npair:C_HEADS, tile])


def _odd_pre(x3, g_pre, w_in, b_i, b_f, *, ts=1024):
    b, s, d = x3.shape
    nqk = C_HEADS * C_DQK
    nv = C_HEADS * LANE
    o_g = 2 * nqk + nv
    wm = jnp.concatenate([w_in[:, :o_g], w_in[:, o_g + 2 * C_HEADS:]], axis=1).astype(BF)
    wsm = w_in[:, o_g:o_g + 2 * C_HEADS]
    bias = jnp.concatenate([b_i, b_f])
    order = jnp.array([g * C_HEADS + h for g in range(2) for h in (*range(0, C_HEADS, 2), *range(1, C_HEADS, 2))])
    wst = wsm.T[order].astype(BF)
    tok = lambda n: pl.BlockSpec((None, ts, n), lambda bi, i: (bi, i, 0))
    vec = lambda n: pl.BlockSpec((1, n), lambda bi, i: (0, 0))
    nck = ts // CHUNK
    return pl.pallas_call(
        functools.partial(_odd_pre_body, ts=ts),
        grid=(b, s // ts),
        in_specs=[tok(d), vec(d), _vmem(), _vmem(), _vmem()],
        out_specs=[tok(nqk), tok(nqk), tok(nv), tok(nv), tok(LANE),
                   pl.BlockSpec((None, nck, C_HEADS // 2, LANE), lambda bi, i: (bi, i, 0, 0))],
        out_shape=[jax.ShapeDtypeStruct((b, s, nqk), BF), jax.ShapeDtypeStruct((b, s, nqk), BF),
                   jax.ShapeDtypeStruct((b, s, nv), BF), jax.ShapeDtypeStruct((b, s, nv), BF),
                   jax.ShapeDtypeStruct((b, s, LANE), F32),
                   jax.ShapeDtypeStruct((b, s // CHUNK, C_HEADS // 2, LANE), F32)],
        compiler_params=_params("parallel", "parallel"),
        name="odd_pre",
    )(x3, g_pre.reshape(1, d), wm, wst, bias[order].reshape(2 * C_HEADS, 1))


def _mlstm_local_body(q_ref, k_ref, colp_ref, rowp_ref,
                      pk_ref, qs_ref, en_ref, se_ref, m_ref, *, nck):
    L = CHUNK
    npair = C_HEADS // 2
    row = lax.broadcasted_iota(jnp.int32, (L, LANE), 0)
    lane = lax.broadcasted_iota(jnp.int32, (L, LANE), 1)
    lo = lane < C_DQK
    causal = row >= (lane & (L - 1))
    spread = [jnp.where(lo, _head_lane(2 * p), _head_lane(2 * p + 1)) for p in range(npair)]
    zero = jnp.zeros((L, LANE), BF)

    @pl.when(pl.program_id(1) == 0)
    def _():
        m_ref[...] = jnp.zeros_like(m_ref)

    m = m_ref[0:1, :]
    g, m_in, wsc = [], [], []
    for ci in range(nck):
        rows = slice(ci * L, (ci + 1) * L)
        colp = colp_ref[rows, :]
        li = jnp.where(lane < C_HEADS, colp, 0.0)
        bc = jnp.where(lane < C_HEADS, pltpu.roll(colp, LANE - C_HEADS, axis=1), 0.0)
        cm = li - bc
        for s in (1, 2, 4, 8, 16, 32):
            cm = jnp.maximum(cm, jnp.where(row >= s, pltpu.roll(cm, s, axis=0), -jnp.inf))
        gi = jnp.maximum(m, cm)
        en_ref[rows, :] = jnp.exp(-bc - gi)
        bl = bc[L - 1:L, :]
        we = bl - bc + li
        m_new = jnp.maximum(bl + m, jnp.max(we, axis=0, keepdims=True))
        sev = jnp.exp(bl + m - m_new)
        for h in range(C_HEADS):
            se_ref[ci, h:h + 1, :] = jnp.broadcast_to(sev[:, _head_lane(h):_head_lane(h) + 1], (1, LANE))
        g.append(gi)
        m_in.append(jnp.broadcast_to(m, (8, LANE)))
        wsc.append(we - m_new)
        m = m_new
    m_ref[...] = jnp.broadcast_to(m, m_ref.shape)

    units = [(ci, p) for ci in range(nck) for p in range(npair)]
    gp = [jnp.take_along_axis(g[ci], spread[p], axis=1) for ci, p in units]
    mp = [jnp.take_along_axis(m_in[ci], spread[p][0:8], axis=1)[0:1] for ci, p in units]
    wp = [jnp.take_along_axis(wsc[ci], spread[p], axis=1) for ci, p in units]

    qk = []
    for ci, p in units:
        rows, ps = slice(ci * L, (ci + 1) * L), slice(p * LANE, (p + 1) * LANE)
        kp = k_ref[rows, ps]
        qk.append(_dot_nt(q_ref[rows, ps], jnp.concatenate([jnp.where(lo, kp, zero), jnp.where(lo, zero, kp)], axis=0)))
    for i, (ci, p) in enumerate(units):
        rows, ps = slice(ci * L, (ci + 1) * L), slice(p * LANE, (p + 1) * LANE)
        pm = (jnp.exp(jnp.where(causal, rowp_ref[ci, p:p + 1, :] - gp[i], -jnp.inf)) * qk[i]).astype(BF)
        pk_ref[ci, p, 0:L, :] = jnp.where(lo, pm, zero)
        pk_ref[ci, p, L:2 * L, :] = jnp.where(lo, zero, pm)
        qs = (q_ref[rows, ps].astype(F32) * jnp.exp(mp[i] - gp[i])).astype(BF)
        qs_ref[ci, p, 0:L, :] = jnp.where(lo, qs, zero)
        qs_ref[ci, p, L:2 * L, :] = jnp.where(lo, zero, qs)
    for i, (ci, p) in enumerate(units):
        rows, ps = slice(ci * L, (ci + 1) * L), slice(p * LANE, (p + 1) * LANE)
        kw = k_ref[rows, ps].astype(F32) * jnp.exp(wp[i])
        kw2 = jnp.concatenate([jnp.where(lo, kw, 0.0), jnp.where(lo, 0.0, kw)], axis=0)
        pk_ref[ci, p, 2 * L:4 * L, :] = kw2.T.astype(BF)


def _mlstm_local(q, k, colp, rowp, *, ts=512):
    b, s, nqk = q.shape
    nck = ts // CHUNK
    npair = C_HEADS // 2
    tok = lambda n: pl.BlockSpec((None, ts, n), lambda bi, i: (bi, i, 0))
    return pl.pallas_call(
        functools.partial(_mlstm_local_body, nck=nck),
        grid=(b, s // ts),
        in_specs=[tok(nqk), tok(nqk), tok(LANE),
                  pl.BlockSpec((None, nck, npair, LANE), lambda bi, i: (bi, i, 0, 0))],
        out_specs=[pl.BlockSpec((None, nck, npair, 4 * CHUNK, LANE), lambda bi, i: (bi, i, 0, 0, 0)),
                   pl.BlockSpec((None, nck, npair, 2 * CHUNK, LANE), lambda bi, i: (bi, i, 0, 0, 0)),
                   tok(LANE),
                   pl.BlockSpec((None, nck, C_HEADS, LANE), lambda bi, i: (bi, i, 0, 0))],
        out_shape=[jax.ShapeDtypeStruct((b, s // CHUNK, npair, 4 * CHUNK, LANE), BF),
                   jax.ShapeDtypeStruct((b, s // CHUNK, npair, 2 * CHUNK, LANE), BF),
                   jax.ShapeDtypeStruct((b, s, LANE), F32),
                   jax.ShapeDtypeStruct((b, s // CHUNK, C_HEADS, LANE), F32)],
        scratch_shapes=[pltpu.VMEM((C_HEADS, LANE), F32)],
        compiler_params=_params("parallel", "arbitrary"),
        name="mlstm_local",
    )(q, k, colp, rowp)


def _mlstm_scan_body(pk_ref, qs_ref, en_ref, se_ref, v_ref, og_ref, ong_ref,
                     y_ref, c_ref, *, nck, bb):
    L = CHUNK
    ones = jnp.ones((L, LANE), BF)

    @pl.when(pl.program_id(1) == 0)
    def _():
        c_ref[...] = jnp.zeros_like(c_ref)

    ch = [(bi, h) for bi in range(bb) for h in range(C_HEADS)]
    pairs = [(bi, p) for bi in range(bb) for p in range(C_HEADS // 2)]
    hsl = lambda h: slice(h * LANE, (h + 1) * LANE)
    srl = lambda h: slice((h % 2) * C_DQK, (h % 2 + 1) * C_DQK)
    for ci in range(nck):
        rows = slice(ci * L, (ci + 1) * L)
        qc = {(bi, p): _dot(qs_ref[bi, ci, p], c_ref[bi, p].astype(BF)) for bi, p in pairs}
        pv = {}
        for bi, p in pairs:
            vst = jnp.concatenate(
                [jnp.concatenate([v_ref[bi, rows, hsl(2 * p + sub)], ones], axis=1) for sub in range(2)], axis=0)
            pv[(bi, p)] = _dot(pk_ref[bi, ci, p], vst)
        for bi, h in ch:
            se = se_ref[bi, ci, h:h + 1, :]
            c_ref[bi, h // 2, srl(h), :] = (c_ref[bi, h // 2, srl(h), :] * jnp.concatenate([se, se], axis=1)
                                            + pv[(bi, h // 2)][2 * L + (h % 2) * L:3 * L + (h % 2) * L])
        res = [qc[(bi, h // 2)][srl(h)] + pv[(bi, h // 2)][srl(h)] for bi, h in ch]
        den = [jnp.maximum(jnp.abs(res[i][:, LANE:]), en_ref[bi, rows, _head_lane(h):_head_lane(h) + 1])
               for i, (bi, h) in enumerate(ch)]
        hh = [_sigmoid(og_ref[bi, rows, hsl(h)].astype(F32)) * (res[i][:, :LANE] / den[i])
              for i, (bi, h) in enumerate(ch)]
        for i, (bi, h) in enumerate(ch):
            y_ref[bi, rows, hsl(h)] = _rms(hh[i], ong_ref[:, hsl(h)]).astype(BF)


def _mlstm_scan(pk, qs, en, se, v, og, onorm_g, *, ts=256, bb=4):
    b, s, nv = v.shape
    bb = min(bb, b)
    nck = ts // CHUNK
    npair = C_HEADS // 2
    tok = lambda n: pl.BlockSpec((bb, ts, n), lambda bi, i: (bi, i, 0))
    return pl.pallas_call(
        functools.partial(_mlstm_scan_body, nck=nck, bb=bb),
        grid=(b // bb, s // ts),
        in_specs=[pl.BlockSpec((bb, nck, npair, 4 * CHUNK, LANE), lambda bi, i: (bi, i, 0, 0, 0)),
                  pl.BlockSpec((bb, nck, npair, 2 * CHUNK, LANE), lambda bi, i: (bi, i, 0, 0, 0)),
                  tok(LANE),
                  pl.BlockSpec((bb, nck, C_HEADS, LANE), lambda bi, i: (bi, i, 0, 0)),
                  tok(nv), tok(nv), pl.BlockSpec((1, nv), lambda bi, i: (0, 0))],
        out_specs=tok(nv),
        out_shape=jax.ShapeDtypeStruct((b, s, nv), BF),
        scratch_shapes=[pltpu.VMEM((bb, npair, LANE, 2 * LANE), F32)],
        compiler_params=_params("parallel", "arbitrary"),
        name="mlstm_scan",
    )(pk, qs, en, se, v, og, onorm_g.reshape(1, nv))


def kernel(x, norm_g, ffn1_gu, ffn1_down, ffn2_gu, ffn2_down, ev_w_in, ev_w_out, ev_a_ln_g, ev_a_ws, ev_a_bs, ev_b_conv, ev_b_a_log, ev_b_dt_bias, ev_b_onorm_g, od_w_in, od_w_out, od_b_i, od_b_f, od_onorm_g):
    b, s, d = x.shape
    x2 = x.reshape(b * s, d)
    depth = norm_g.shape[0]
    for layer in range(depth):
        ng = norm_g[layer]
        j = layer // 2
        x2 = _ffn(x2, ng[0], _to_bf16(ffn1_gu, layer), _to_bf16(ffn1_down, layer), ng[1])
        x3 = x2.reshape(b, s, d)
        if layer % 2 == 0:
            ya, q, k, v, gate, colp, rowp = _even_pre(
                x3, ng[2], ev_w_in[j], ev_a_ln_g[j], ev_a_ws[j], ev_a_bs[j], ev_b_conv[j],
                ev_b_a_log[j], ev_b_dt_bias[j])
            u, w, qg, attn, kdt, dec = _gdn_wy(q, k, v, colp, rowp)
            yb = _gdn_scan(u, w, qg, attn, kdt, dec, gate, ev_b_onorm_g[j])
            mix = (ya.reshape(b * s, -1), 0, yb.reshape(b * s, -1), 0, ev_w_out[j])
        else:
            q, k, v, og, colp, rowp = _odd_pre(x3, ng[2], od_w_in[j], od_b_i[j], od_b_f[j])
            pk, qs, en, se = _mlstm_local(q, k, colp, rowp)
            y = _mlstm_scan(pk, qs, en, se, v, og, od_onorm_g[j]).reshape(b * s, -1)
            mix = (y, 0, y, 1, od_w_out[j])
        x2 = _post_ffn(*mix, ng[3], x2, ng[4], _to_bf16(ffn2_gu, layer), _to_bf16(ffn2_down, layer), ng[5])
    return x2.reshape(b, s, d)
```

```python
import functools

import jax
import jax.numpy as jnp
from jax import lax
from jax.experimental import pallas as pl
from jax.experimental.pallas import tpu as pltpu

F32 = jnp.float32
BF = jnp.bfloat16
HI = lax.Precision.HIGHEST

EPS = 1e-6
LANE = 128
CHUNK = 64
A_GROUPS, A_CHUNK = 4, 128
B_HEADS, B_CONV = 4, 4
C_HEADS, C_DQK = 8, 64
GATE_CAP = 15.0
VMEM_LIMIT = 56 * 1024 * 1024


def _dot(a, b, prec=None):
    return jnp.dot(a, b, preferred_element_type=F32, precision=prec)


def _dot_nt(a, b, prec=None):
    return lax.dot_general(a, b, (((1,), (1,)), ((), ())), preferred_element_type=F32, precision=prec)


def _rms(xf, g):
    return xf * lax.rsqrt(jnp.mean(xf * xf, axis=-1, keepdims=True) + EPS) * g


def _sigmoid(x):
    return 1.0 / (1.0 + jnp.exp(-x))


def _silu(x):
    return x * _sigmoid(x)


def _gelu(x):
    return 0.5 * x * (1.0 + jnp.tanh(0.7978845608028654 * (x + 0.044715 * (x * x * x))))


def _softplus(x):
    return jnp.maximum(x, 0.0) + jnp.log1p(jnp.exp(-jnp.abs(x)))


def _chunk_tri(transpose):
    r = lax.broadcasted_iota(jnp.int32, (LANE, LANE), 1 if transpose else 0)
    c = lax.broadcasted_iota(jnp.int32, (LANE, LANE), 0 if transpose else 1)
    same = lax.shift_right_logical(r, 6) == lax.shift_right_logical(c, 6)
    return jnp.where(same, jnp.where(c <= r, 1.0, 0.0), 0.0).astype(BF)


def _split3(x):
    p1 = x.astype(BF).astype(F32)
    r1 = x - p1
    p2 = r1.astype(BF).astype(F32)
    return p1, p2, (r1 - p2).astype(BF).astype(F32)


SPLIT_LANES = 16


def _chunk_cumsum_cols(x):
    lane = lax.broadcasted_iota(jnp.int32, x.shape, 1)
    p1, p2, p3 = _split3(jnp.where(lane < SPLIT_LANES, x, 0.0))
    packed = (p1 + pltpu.roll(p2, SPLIT_LANES, axis=1) + pltpu.roll(p3, 2 * SPLIT_LANES, axis=1)).astype(BF)
    tri = _chunk_tri(False)
    res = jnp.concatenate([_dot(tri, packed[t:t + LANE]) for t in range(0, x.shape[0], LANE)], axis=0)
    return res + pltpu.roll(res, LANE - SPLIT_LANES, axis=1) + pltpu.roll(res, LANE - 2 * SPLIT_LANES, axis=1)


def _chunk_cumsum_rows(x):
    r = x.shape[0]
    p1, p2, p3 = _split3(x)
    stacked = jnp.concatenate([p1, p2, p3, jnp.zeros_like(p1)], axis=0).astype(BF)
    tri_t = _chunk_tri(True)
    res = jnp.concatenate([_dot(stacked[:, t:t + LANE], tri_t) for t in range(0, x.shape[1], LANE)], axis=1)
    return res[0:r] + res[r:2 * r] + res[2 * r:3 * r]


def _params(*sem):
    return pltpu.CompilerParams(dimension_semantics=sem, vmem_limit_bytes=VMEM_LIMIT)


def _vmem():
    return pl.BlockSpec(memory_space=pltpu.VMEM)


def _cast_body(w_ref, o_ref):
    o_ref[...] = w_ref[...].astype(BF)


def _to_bf16(w_stack, layer, *, steps=8):
    _, rows, cols = w_stack.shape
    tr = rows // steps
    return pl.pallas_call(
        _cast_body,
        grid=(steps,),
        in_specs=[pl.BlockSpec((None, tr, cols), lambda i: (layer, i, 0))],
        out_specs=pl.BlockSpec((tr, cols), lambda i: (i, 0)),
        out_shape=jax.ShapeDtypeStruct((rows, cols), BF),
        compiler_params=_params("parallel"),
        name="cast_bf16",
    )(w_stack)


def _swiglu_step(x, gpre_ref, wgu_ref, wd_ref, gpost_ref, o_ref, acc_ref, tf):
    xn = _rms(x, gpre_ref[...]).astype(BF)
    f = wd_ref.shape[0]

    def gate_up(j):
        return (_dot(xn, wgu_ref[:, j * tf:(j + 1) * tf]), _dot(xn, wgu_ref[:, f + j * tf:f + (j + 1) * tf]))

    n_chunks = f // tf
    gu = gate_up(0)
    for j in range(n_chunks):
        g, u = gu
        if j + 1 < n_chunks:
            gu = gate_up(j + 1)
        d = _dot((_silu(g) * u).astype(BF), wd_ref[j * tf:(j + 1) * tf, :])
        if j == 0:
            acc_ref[...] = d
        else:
            acc_ref[...] += d
    o_ref[...] = x + _rms(acc_ref[...], 0.5 * gpost_ref[...])


def _ffn_body(x_ref, gpre_ref, wgu_ref, wd_ref, gpost_ref, o_ref, acc_ref, *, tf):
    _swiglu_step(x_ref[...], gpre_ref, wgu_ref, wd_ref, gpost_ref, o_ref, acc_ref, tf)


def _ffn(x2, g_pre, w_gu, w_down, g_post, *, tm=1024, tf=256):
    m, d = x2.shape
    row = pl.BlockSpec((tm, d), lambda i: (i, 0))
    vec = pl.BlockSpec((1, d), lambda i: (0, 0))
    return pl.pallas_call(
        functools.partial(_ffn_body, tf=tf),
        grid=(m // tm,),
        in_specs=[row, vec, _vmem(), _vmem(), vec],
        out_specs=row,
        out_shape=jax.ShapeDtypeStruct((m, d), F32),
        scratch_shapes=[pltpu.VMEM((tm, d), F32)],
        compiler_params=_params("parallel"),
        name="ffn",
    )(x2, g_pre.reshape(1, d), w_gu, w_down, g_post.reshape(1, d))


def _post_ffn_body(y1_ref, y2_ref, w1_ref, w2_ref, gmix_ref, x_ref, gpre_ref, wgu_ref, wd_ref, gpost_ref,
                   o_ref, acc_ref, *, tf):
    y = _dot(y1_ref[...], w1_ref[...]) + _dot(y2_ref[...], w2_ref[...])
    x = x_ref[...] + _rms(y, gmix_ref[...])
    _swiglu_step(x, gpre_ref, wgu_ref, wd_ref, gpost_ref, o_ref, acc_ref, tf)


def _post_ffn(y1, c1, y2, c2, w_out, g_mix, x2, g_pre, w_gu, w_down, g_post, *, tm=1024, tf=256):
    m, d = x2.shape
    half = w_out.shape[0] // 2
    w = w_out.astype(BF)
    row = pl.BlockSpec((tm, d), lambda i: (i, 0))
    vec = pl.BlockSpec((1, d), lambda i: (0, 0))
    return pl.pallas_call(
        functools.partial(_post_ffn_body, tf=tf),
        grid=(m // tm,),
        in_specs=[pl.BlockSpec((tm, half), lambda i: (i, c1)), pl.BlockSpec((tm, half), lambda i: (i, c2)),
                  _vmem(), _vmem(), vec, row, vec, _vmem(), _vmem(), vec],
        out_specs=row,
        out_shape=jax.ShapeDtypeStruct((m, d), F32),
        scratch_shapes=[pltpu.VMEM((tm, d), F32)],
        compiler_params=_params("parallel"),
        name="post_ffn",
    )(y1, y2, w[:half], w[half:], g_mix.reshape(1, d), x2, g_pre.reshape(1, d),
      w_gu, w_down, g_post.reshape(1, d))


def _even_pre_body(x_ref, gpre_ref, wm_ref, wst_ref, lng_ref, aws_ref, bst_ref, cw_ref,
                   alog_c_ref, dtb_c_ref,
                   y_ref, q_ref, k_ref, v_ref, gate_ref, colp_ref, rowp_ref, zbuf_ref, *, ts):
    aw = A_GROUPS * LANE
    cq = 3 * B_HEADS * LANE
    xn = _rms(x_ref[...], gpre_ref[...]).astype(BF)

    zr = _dot_nt(wst_ref[...], xn)

    @pl.when(pl.program_id(1) == 0)
    def _():
        zbuf_ref[0:8, :] = jnp.zeros((8, cq), F32)

    zbuf_ref[8:8 + ts, :] = _dot(xn, wm_ref[:, 2 * aw:2 * aw + cq])
    zu = _dot(xn, wm_ref[:, 0:aw])
    zv = _dot(xn, wm_ref[:, aw:2 * aw])

    grow = -jnp.exp(alog_c_ref[...]) * _softplus(zr + dtb_c_ref[...])
    sub = lax.broadcasted_iota(jnp.int32, (8, ts), 0)
    gates = jnp.where(sub < B_HEADS, _sigmoid(zr), _chunk_cumsum_rows(grow))
    for ci in range(ts // CHUNK):
        rowp_ref[ci] = gates[:, ci * CHUNK:(ci + 1) * CHUNK]
    padded = jnp.concatenate([gates, jnp.zeros((LANE - 8, ts), F32)], axis=0)
    for t in range(0, ts, LANE):
        colp_ref[t:t + LANE, :] = padded[:, t:t + LANE].T
    gate_ref[...] = _dot(xn, wm_ref[:, 2 * aw + cq:]).astype(BF)

    u = _gelu(zu)
    v = _gelu(zv)
    r = lax.broadcasted_iota(jnp.int32, (A_CHUNK, A_CHUNK), 0)
    c = lax.broadcasted_iota(jnp.int32, (A_CHUNK, A_CHUNK), 1)
    for g in range(A_GROUPS):
        gs = slice(g * LANE, (g + 1) * LANE)
        vg = v[:, gs]
        d = vg - jnp.mean(vg, axis=-1, keepdims=True)
        var = jnp.mean(d * d, axis=-1, keepdims=True)
        vn = (d * lax.rsqrt(var + EPS) * lng_ref[:, gs]).astype(BF)
        w = jnp.where(r >= c, aws_ref[g], 0.0).astype(BF)
        bcol = bst_ref[:, g:g + 1]
        for ci in range(ts // A_CHUNK):
            rs = slice(ci * A_CHUNK, (ci + 1) * A_CHUNK)
            mixed = _dot(w, vn[rs]) + bcol
            y_ref[rs, gs] = (u[rs, gs] * mixed).astype(BF)

    for t in range(3 * B_HEADS):
        cs = slice(t * LANE, (t + 1) * LANE)
        zz = zbuf_ref[:, cs]
        sz = pltpu.roll(zz, 1, axis=0)
        near = cw_ref[3:4, cs] * zz + cw_ref[2:3, cs] * sz
        far = cw_ref[1:2, cs] * zz + cw_ref[0:1, cs] * sz
        a = _silu((near + pltpu.roll(far, 2, axis=0))[8:])
        if t < 2 * B_HEADS:
            a = a * lax.rsqrt(jnp.sum(a * a, axis=-1, keepdims=True) + EPS)
        if t < B_HEADS:
            q_ref[:, cs] = (a * (LANE ** -0.5)).astype(BF)
        elif t < 2 * B_HEADS:
            k_ref[:, slice((t - B_HEADS) * LANE, (t - B_HEADS + 1) * LANE)] = a.astype(BF)
        else:
            v_ref[:, slice((t - 2 * B_HEADS) * LANE, (t - 2 * B_HEADS + 1) * LANE)] = a.astype(BF)
    zbuf_ref[0:8, :] = zbuf_ref[ts:ts + 8, :]


def _even_pre(x3, g_pre, w_in, a_ln_g, a_ws, a_bs, b_conv, a_log, dt_bias, *, ts=1024):
    assert B_CONV == 4 and b_conv.shape[0] == B_CONV
    b, s, d = x3.shape
    aw = A_GROUPS * LANE
    nq = 3 * B_HEADS * LANE
    o_beta = 2 * aw + nq
    wm = jnp.concatenate([w_in[:, :o_beta], w_in[:, o_beta + 2 * B_HEADS:]], axis=1).astype(BF)
    wst = w_in[:, o_beta:o_beta + 2 * B_HEADS].T.astype(BF)
    sub_pad = lambda p: jnp.pad(p, (B_HEADS, 0)).reshape(2 * B_HEADS, 1)
    grid = (b, s // ts)
    tok = lambda w: pl.BlockSpec((None, ts, w), lambda bi, i: (bi, i, 0))
    vec = lambda w: pl.BlockSpec((1, w), lambda bi, i: (0, 0))
    nck = ts // CHUNK
    outs = pl.pallas_call(
        functools.partial(_even_pre_body, ts=ts),
        grid=grid,
        in_specs=[tok(d), vec(d), _vmem(), _vmem(), vec(aw), _vmem(), _vmem(), _vmem(), _vmem(), _vmem()],
        out_specs=[tok(aw), tok(aw), tok(aw), tok(aw), tok(aw), tok(LANE),
                   pl.BlockSpec((None, nck, 8, CHUNK), lambda bi, i: (bi, i, 0, 0))],
        out_shape=[jax.ShapeDtypeStruct((b, s, aw), BF)] * 5
        + [jax.ShapeDtypeStruct((b, s, LANE), F32), jax.ShapeDtypeStruct((b, s // CHUNK, 8, CHUNK), F32)],
        scratch_shapes=[pltpu.VMEM((ts + 8, nq), F32)],
        compiler_params=_params("parallel", "arbitrary"),
        name="even_pre",
    )(x3, g_pre.reshape(1, d), wm, wst, a_ln_g.reshape(1, aw), a_ws, a_bs.T, b_conv,
      sub_pad(a_log), sub_pad(dt_bias))
    return outs


def _gdn_wy_body(q_ref, k_ref, v_ref, colp_ref, rowp_ref,
                 u_ref, w_ref, qg_ref, attn_ref, kdt_ref, dec_ref, *, nck, cg):
    L = CHUNK
    r = lax.broadcasted_iota(jnp.int32, (L, L), 0)
    c = lax.broadcasted_iota(jnp.int32, (L, L), 1)
    causal = r >= c
    strict = r > c
    eye = jnp.where(r == c, 1.0, 0.0).astype(F32)
    shr = lax.shift_right_logical
    diag8 = jnp.where(shr(r, 3) == shr(c, 3), 1.0, 0.0).astype(F32)
    off = [jnp.where((shr(r, t + 1) == shr(c, t + 1)) & (shr(r, t) > shr(c, t)), 1.0, 0.0).astype(F32)
           for t in (3, 4, 5)]

    def mm(a, b):
        return _dot(a.astype(BF), b.astype(BF))

    def group(units):
        n = range(len(units))
        rows = [slice(ci * L, (ci + 1) * L) for ci, _ in units]
        hs = [slice(h * LANE, (h + 1) * LANE) for _, h in units]
        kh = [k_ref[rows[i], hs[i]] for i in n]
        khf = [kh[i].astype(F32) for i in n]
        beta = [colp_ref[rows[i], h:h + 1] for i, (_, h) in enumerate(units)]
        gc = [colp_ref[rows[i], B_HEADS + h:B_HEADS + h + 1] for i, (_, h) in enumerate(units)]
        decay = [jnp.exp(jnp.where(causal, gc[i] - rowp_ref[ci, B_HEADS + h:B_HEADS + h + 1, :], -jnp.inf))
                 for i, (ci, h) in enumerate(units)]
        kb = [khf[i] * beta[i] for i in n]
        kk = [_dot_nt(kb[i].astype(BF), kh[i]) for i in n]
        qk = [_dot_nt(q_ref[rows[i], hs[i]], kh[i]) for i in n]
        low = [jnp.where(strict, kk[i] * decay[i], 0.0) for i in n]
        for i, (_, h) in enumerate(units):
            attn_ref[h, rows[i], :] = jnp.where(causal, qk[i] * decay[i], 0.0).astype(BF)
        pw = [-(low[i] * diag8) for i in n]
        inv = [eye + pw[i] for i in n]
        for _ in range(2):
            pw = [mm(pw[i], pw[i]) for i in n]
            t = [mm(inv[i], pw[i]) for i in n]
            inv = [inv[i] + t[i] for i in n]
        for m in off:
            t = [mm(low[i] * m, inv[i]) for i in n]
            t = [mm(inv[i], t[i]) for i in n]
            inv = [inv[i] - t[i] for i in n]
        eg = [jnp.exp(gc[i]) for i in n]
        sol = [mm(inv[i], jnp.concatenate([v_ref[rows[i], hs[i]].astype(F32) * beta[i], kb[i] * eg[i]], axis=1))
               for i in n]
        for i, (ci, h) in enumerate(units):
            u_ref[rows[i], hs[i]] = sol[i][:, :LANE]
            w_ref[rows[i], hs[i]] = sol[i][:, LANE:].astype(BF)
            qg_ref[rows[i], hs[i]] = (q_ref[rows[i], hs[i]].astype(F32) * eg[i]).astype(BF)
            gl = gc[i][L - 1:L, :]
            kd = khf[i] * jnp.exp(gl - gc[i])
            kdt = jnp.concatenate([kd, jnp.zeros_like(kd)], axis=0).T
            kdt_ref[ci, h] = kdt[:, :L].astype(BF)
            dec_ref[ci, h:h + 1, :] = jnp.broadcast_to(jnp.exp(gl), (1, LANE))

    for c0 in range(0, nck, cg):
        group([(ci, h) for ci in range(c0, c0 + cg) for h in range(B_HEADS)])


def _gdn_wy(q, k, v, colp, rowp, *, ts=512, cg=8):
    b, s, w = q.shape
    nck = ts // CHUNK
    grid = (b, s // ts)
    tok = lambda n: pl.BlockSpec((None, ts, n), lambda bi, i: (bi, i, 0))
    return pl.pallas_call(
        functools.partial(_gdn_wy_body, nck=nck, cg=cg),
        grid=grid,
        in_specs=[tok(w), tok(w), tok(w), tok(LANE),
                  pl.BlockSpec((None, nck, 8, CHUNK), lambda bi, i: (bi, i, 0, 0))],
        out_specs=[tok(w), tok(w), tok(w),
                   pl.BlockSpec((None, B_HEADS, ts, CHUNK), lambda bi, i: (bi, 0, i, 0)),
                   pl.BlockSpec((None, nck, B_HEADS, LANE, CHUNK), lambda bi, i: (bi, i, 0, 0, 0)),
                   pl.BlockSpec((None, nck, B_HEADS, LANE), lambda bi, i: (bi, i, 0, 0))],
        out_shape=[jax.ShapeDtypeStruct((b, s, w), F32), jax.ShapeDtypeStruct((b, s, w), BF),
                   jax.ShapeDtypeStruct((b, s, w), BF),
                   jax.ShapeDtypeStruct((b, B_HEADS, s, CHUNK), BF),
                   jax.ShapeDtypeStruct((b, s // CHUNK, B_HEADS, LANE, CHUNK), BF),
                   jax.ShapeDtypeStruct((b, s // CHUNK, B_HEADS, LANE), F32)],
        compiler_params=_params("parallel", "parallel"),
        name="gdn_wy",
    )(q, k, v, colp, rowp)


def _gdn_scan_body(u_ref, w_ref, qg_ref, attn_ref, kdt_ref, dec_ref, gate_ref, ong_ref,
                   y_ref, s_ref, *, nck, bb):
    L = CHUNK

    @pl.when(pl.program_id(1) == 0)
    def _():
        s_ref[...] = jnp.zeros_like(s_ref)

    ch = [(bi, h, slice(h * LANE, (h + 1) * LANE)) for bi in range(bb) for h in range(B_HEADS)]
    for ci in range(nck):
        rows = slice(ci * L, (ci + 1) * L)
        sb = [s_ref[bi, h].astype(BF) for bi, h, _ in ch]
        ws = [_dot(w_ref[bi, rows, hs], sb[i]) for i, (bi, _, hs) in enumerate(ch)]
        qs = [_dot(qg_ref[bi, rows, hs], sb[i]) for i, (bi, _, hs) in enumerate(ch)]
        vb = [(u_ref[bi, rows, hs] - ws[i]).astype(BF) for i, (bi, _, hs) in enumerate(ch)]
        upd = [_dot(kdt_ref[bi, ci, h], vb[i]) for i, (bi, h, _) in enumerate(ch)]
        av = [_dot(attn_ref[bi, h, rows, :], vb[i]) for i, (bi, h, _) in enumerate(ch)]
        for i, (bi, h, hs) in enumerate(ch):
            s_ref[bi, h] = s_ref[bi, h] * dec_ref[bi, ci, h:h + 1, :] + upd[i]
        for i, (bi, h, hs) in enumerate(ch):
            on = _rms(qs[i] + av[i], ong_ref[...])
            y_ref[bi, rows, hs] = (on * _silu(gate_ref[bi, rows, hs].astype(F32))).astype(BF)


def _gdn_scan(u, w, qg, attn, kdt, dec, gate, onorm_g, *, ts=256, bb=8):
    b, s, wd = u.shape
    bb = min(bb, b)
    nck = ts // CHUNK
    tok = lambda n: pl.BlockSpec((bb, ts, n), lambda bi, i: (bi, i, 0))
    return pl.pallas_call(
        functools.partial(_gdn_scan_body, nck=nck, bb=bb),
        grid=(b // bb, s // ts),
        in_specs=[tok(wd), tok(wd), tok(wd),
                  pl.BlockSpec((bb, B_HEADS, ts, CHUNK), lambda bi, i: (bi, 0, i, 0)),
                  pl.BlockSpec((bb, nck, B_HEADS, LANE, CHUNK), lambda bi, i: (bi, i, 0, 0, 0)),
                  pl.BlockSpec((bb, nck, B_HEADS, LANE), lambda bi, i: (bi, i, 0, 0)),
                  tok(wd), pl.BlockSpec((1, LANE), lambda bi, i: (0, 0))],
        out_specs=tok(wd),
        out_shape=jax.ShapeDtypeStruct((b, s, wd), BF),
        scratch_shapes=[pltpu.VMEM((bb, B_HEADS, LANE, LANE), F32)],
        compiler_params=_params("parallel", "arbitrary"),
        name="gdn_scan",
    )(u, w, qg, attn, kdt, dec, gate, onorm_g.reshape(1, LANE))


def _head_lane(h):
    return h // 2 + (C_HEADS // 2) * (h % 2)


def _odd_pre_body(x_ref, gpre_ref, wm_ref, wst_ref, bc_ref,
                  q_ref, k_ref, v_ref, og_ref, colp_ref, rowp_ref, *, ts):
    nqk = C_HEADS * C_DQK
    nv = C_HEADS * LANE
    xn = _rms(x_ref[...], gpre_ref[...]).astype(BF)
    zr = _dot_nt(wst_ref[...], xn)
    q_ref[...] = _dot(xn, wm_ref[:, 0:nqk]).astype(BF)
    k_ref[...] = (_dot(xn, wm_ref[:, nqk:2 * nqk]) * (C_DQK ** -0.5)).astype(BF)
    v_ref[...] = _dot(xn, wm_ref[:, 2 * nqk:2 * nqk + nv]).astype(BF)

    capr = GATE_CAP * jnp.tanh((zr + bc_ref[...]) * (1.0 / GATE_CAP))
    bcr = _chunk_cumsum_rows(-_softplus(-capr))
    og_ref[...] = _dot(xn, wm_ref[:, 2 * nqk + nv:]).astype(BF)
    sub = lax.broadcasted_iota(jnp.int32, (2 * C_HEADS, ts), 0)
    padded = jnp.concatenate([jnp.where(sub < C_HEADS, capr, bcr), jnp.zeros((LANE - 2 * C_HEADS, ts), F32)], axis=0)
    for t in range(0, ts, LANE):
        colp_ref[t:t + LANE, :] = padded[:, t:t + LANE].T
    brow = capr[0:C_HEADS] - bcr[C_HEADS:2 * C_HEADS]
    right = pltpu.roll(brow, CHUNK, axis=1)
    left = pltpu.roll(brow, ts - CHUNK, axis=1)
    npair = C_HEADS // 2
    lo = lax.broadcasted_iota(jnp.int32, (npair, LANE), 1) < CHUNK
    for ci in range(ts // CHUNK):
        tile = slice((ci // 2) * LANE, (ci // 2 + 1) * LANE)
        if ci % 2 == 0:
            rowp_ref[ci] = jnp.where(lo, brow[0:npair, tile], right[npair:C_HEADS, tile])
        else:
            rowp_ref[ci] = jnp.where(lo, left[0:npair, tile], brow[npair:C_HEADS, tile])


def _odd_pre(x3, g_pre, w_in, b_i, b_f, *, ts=1024):
    b, s, d = x3.shape
    nqk = C_HEADS * C_DQK
    nv = C_HEADS * LANE
    o_g = 2 * nqk + nv
    wm = jnp.concatenate([w_in[:, :o_g], w_in[:, o_g + 2 * C_HEADS:]], axis=1).astype(BF)
    wsm = w_in[:, o_g:o_g + 2 * C_HEADS]
    bias = jnp.concatenate([b_i, b_f])
    order = jnp.array([g * C_HEADS + h for g in range(2) for h in (*range(0, C_HEADS, 2), *range(1, C_HEADS, 2))])
    wst = wsm.T[order].astype(BF)
    tok = lambda n: pl.BlockSpec((None, ts, n), lambda bi, i: (bi, i, 0))
    vec = lambda n: pl.BlockSpec((1, n), lambda bi, i: (0, 0))
    nck = ts // CHUNK
    return pl.pallas_call(
        functools.partial(_odd_pre_body, ts=ts),
        grid=(b, s // ts),
        in_specs=[tok(d), vec(d), _vmem(), _vmem(), _vmem()],
        out_specs=[tok(nqk), tok(nqk), tok(nv), tok(nv), tok(LANE),
                   pl.BlockSpec((None, nck, C_HEADS // 2, LANE), lambda bi, i: (bi, i, 0, 0))],
        out_shape=[jax.ShapeDtypeStruct((b, s, nqk), BF), jax.ShapeDtypeStruct((b, s, nqk), BF),
                   jax.ShapeDtypeStruct((b, s, nv), BF), jax.ShapeDtypeStruct((b, s, nv), BF),
                   jax.ShapeDtypeStruct((b, s, LANE), F32),
                   jax.ShapeDtypeStruct((b, s // CHUNK, C_HEADS // 2, LANE), F32)],
        compiler_params=_params("parallel", "parallel"),
        name="odd_pre",
    )(x3, g_pre.reshape(1, d), wm, wst, bias[order].reshape(2 * C_HEADS, 1))


def _mlstm_local_body(q_ref, k_ref, colp_ref, rowp_ref,
                      pk_ref, qs_ref, en_ref, se_ref, m_ref, *, nck):
    L = CHUNK
    npair = C_HEADS // 2
    row = lax.broadcasted_iota(jnp.int32, (L, LANE), 0)
    lane = lax.broadcasted_iota(jnp.int32, (L, LANE), 1)
    lo = lane < C_DQK
    causal = row >= (lane & (L - 1))
    spread = [jnp.where(lo, _head_lane(2 * p), _head_lane(2 * p + 1)) for p in range(npair)]
    zero = jnp.zeros((L, LANE), BF)

    @pl.when(pl.program_id(1) == 0)
    def _():
        m_ref[...] = jnp.zeros_like(m_ref)

    m = m_ref[0:1, :]
    g, m_in, wsc = [], [], []
    for ci in range(nck):
        rows = slice(ci * L, (ci + 1) * L)
        colp = colp_ref[rows, :]
        li = jnp.where(lane < C_HEADS, colp, 0.0)
        bc = jnp.where(lane < C_HEADS, pltpu.roll(colp, LANE - C_HEADS, axis=1), 0.0)
        cm = li - bc
        for s in (1, 2, 4, 8, 16, 32):
            cm = jnp.maximum(cm, jnp.where(row >= s, pltpu.roll(cm, s, axis=0), -jnp.inf))
        gi = jnp.maximum(m, cm)
        en_ref[rows, :] = jnp.exp(-bc - gi)
        bl = bc[L - 1:L, :]
        we = bl - bc + li
        m_new = jnp.maximum(bl + m, jnp.max(we, axis=0, keepdims=True))
        sev = jnp.exp(bl + m - m_new)
        for h in range(C_HEADS):
            se_ref[ci, h:h + 1, :] = jnp.broadcast_to(sev[:, _head_lane(h):_head_lane(h) + 1], (1, LANE))
        g.append(gi)
        m_in.append(jnp.broadcast_to(m, (8, LANE)))
        wsc.append(we - m_new)
        m = m_new
    m_ref[...] = jnp.broadcast_to(m, m_ref.shape)

    units = [(ci, p) for ci in range(nck) for p in range(npair)]
    gp = [jnp.take_along_axis(g[ci], spread[p], axis=1) for ci, p in units]
    mp = [jnp.take_along_axis(m_in[ci], spread[p][0:8], axis=1)[0:1] for ci, p in units]
    wp = [jnp.take_along_axis(wsc[ci], spread[p], axis=1) for ci, p in units]

    qk = []
    for ci, p in units:
        rows, ps = slice(ci * L, (ci + 1) * L), slice(p * LANE, (p + 1) * LANE)
        kp = k_ref[rows, ps]
        qk.append(_dot_nt(q_ref[rows, ps], jnp.concatenate([jnp.where(lo, kp, zero), jnp.where(lo, zero, kp)], axis=0)))
    for i, (ci, p) in enumerate(units):
        rows, ps = slice(ci * L, (ci + 1) * L), slice(p * LANE, (p + 1) * LANE)
        pm = (jnp.exp(jnp.where(causal, rowp_ref[ci, p:p + 1, :] - gp[i], -jnp.inf)) * qk[i]).astype(BF)
        pk_ref[ci, p, 0:L, :] = jnp.where(lo, pm, zero)
        pk_ref[ci, p, L:2 * L, :] = jnp.where(lo, zero, pm)
        qs = (q_ref[rows, ps].astype(F32) * jnp.exp(mp[i] - gp[i])).astype(BF)
        qs_ref[ci, p, 0:L, :] = jnp.where(lo, qs, zero)
        qs_ref[ci, p, L:2 * L, :] = jnp.where(lo, zero, qs)
    for i, (ci, p) in enumerate(units):
        rows, ps = slice(ci * L, (ci + 1) * L), slice(p * LANE, (p + 1) * LANE)
        kw = k_ref[rows, ps].astype(F32) * jnp.exp(wp[i])
        kw2 = jnp.concatenate([jnp.where(lo, kw, 0.0), jnp.where(lo, 0.0, kw)], axis=0)
        pk_ref[ci, p, 2 * L:4 * L, :] = kw2.T.astype(BF)


def _mlstm_local(q, k, colp, rowp, *, ts=512):
    b, s, nqk = q.shape
    nck = ts // CHUNK
    npair = C_HEADS // 2
    tok = lambda n: pl.BlockSpec((None, ts, n), lambda bi, i: (bi, i, 0))
    return pl.pallas_call(
        functools.partial(_mlstm_local_body, nck=nck),
        grid=(b, s // ts),
        in_specs=[tok(nqk), tok(nqk), tok(LANE),
                  pl.BlockSpec((None, nck, npair, LANE), lambda bi, i: (bi, i, 0, 0))],
        out_specs=[pl.BlockSpec((None, nck, npair, 4 * CHUNK, LANE), lambda bi, i: (bi, i, 0, 0, 0)),
                   pl.BlockSpec((None, nck, npair, 2 * CHUNK, LANE), lambda bi, i: (bi, i, 0, 0, 0)),
                   tok(LANE),
                   pl.BlockSpec((None, nck, C_HEADS, LANE), lambda bi, i: (bi, i, 0, 0))],
        out_shape=[jax.ShapeDtypeStruct((b, s // CHUNK, npair, 4 * CHUNK, LANE), BF),
                   jax.ShapeDtypeStruct((b, s // CHUNK, npair, 2 * CHUNK, LANE), BF),
                   jax.ShapeDtypeStruct((b, s, LANE), F32),
                   jax.ShapeDtypeStruct((b, s // CHUNK, C_HEADS, LANE), F32)],
        scratch_shapes=[pltpu.VMEM((C_HEADS, LANE), F32)],
        compiler_params=_params("parallel", "arbitrary"),
        name="mlstm_local",
    )(q, k, colp, rowp)


def _mlstm_scan_body(pk_ref, qs_ref, en_ref, se_ref, v_ref, og_ref, ong_ref,
                     y_ref, c_ref, *, nck, bb):
    L = CHUNK
    ones = jnp.ones((L, LANE), BF)

    @pl.when(pl.program_id(1) == 0)
    def _():
        c_ref[...] = jnp.zeros_like(c_ref)

    ch = [(bi, h) for bi in range(bb) for h in range(C_HEADS)]
    pairs = [(bi, p) for bi in range(bb) for p in range(C_HEADS // 2)]
    hsl = lambda h: slice(h * LANE, (h + 1) * LANE)
    srl = lambda h: slice((h % 2) * C_DQK, (h % 2 + 1) * C_DQK)
    for ci in range(nck):
        rows = slice(ci * L, (ci + 1) * L)
        qc = {(bi, p): _dot(qs_ref[bi, ci, p], c_ref[bi, p].astype(BF)) for bi, p in pairs}
        pv = {}
        for bi, p in pairs:
            vst = jnp.concatenate(
                [jnp.concatenate([v_ref[bi, rows, hsl(2 * p + sub)], ones], axis=1) for sub in range(2)], axis=0)
            pv[(bi, p)] = _dot(pk_ref[bi, ci, p], vst)
        for bi, h in ch:
            se = se_ref[bi, ci, h:h + 1, :]
            c_ref[bi, h // 2, srl(h), :] = (c_ref[bi, h // 2, srl(h), :] * jnp.concatenate([se, se], axis=1)
                                            + pv[(bi, h // 2)][2 * L + (h % 2) * L:3 * L + (h % 2) * L])
        res = [qc[(bi, h // 2)][srl(h)] + pv[(bi, h // 2)][srl(h)] for bi, h in ch]
        den = [jnp.maximum(jnp.abs(res[i][:, LANE:]), en_ref[bi, rows, _head_lane(h):_head_lane(h) + 1])
               for i, (bi, h) in enumerate(ch)]
        hh = [_sigmoid(og_ref[bi, rows, hsl(h)].astype(F32)) * (res[i][:, :LANE] / den[i])
              for i, (bi, h) in enumerate(ch)]
        for i, (bi, h) in enumerate(ch):
            y_ref[bi, rows, hsl(h)] = _rms(hh[i], ong_ref[:, hsl(h)]).astype(BF)


def _mlstm_scan(pk, qs, en, se, v, og, onorm_g, *, ts=256, bb=4):
    b, s, nv = v.shape
    bb = min(bb, b)
    nck = ts // CHUNK
    npair = C_HEADS // 2
    tok = lambda n: pl.BlockSpec((bb, ts, n), lambda bi, i: (bi, i, 0))
    return pl.pallas_call(
        functools.partial(_mlstm_scan_body, nck=nck, bb=bb),
        grid=(b // bb, s // ts),
        in_specs=[pl.BlockSpec((bb, nck, npair, 4 * CHUNK, LANE), lambda bi, i: (bi, i, 0, 0, 0)),
                  pl.BlockSpec((bb, nck, npair, 2 * CHUNK, LANE), lambda bi, i: (bi, i, 0, 0, 0)),
                  tok(LANE),
                  pl.BlockSpec((bb, nck, C_HEADS, LANE), lambda bi, i: (bi, i, 0, 0)),
                  tok(nv), tok(nv), pl.BlockSpec((1, nv), lambda bi, i: (0, 0))],
        out_specs=tok(nv),
        out_shape=jax.ShapeDtypeStruct((b, s, nv), BF),
        scratch_shapes=[pltpu.VMEM((bb, npair, LANE, 2 * LANE), F32)],
        compiler_params=_params("parallel", "arbitrary"),
        name="mlstm_scan",
    )(pk, qs, en, se, v, og, onorm_g.reshape(1, nv))


def kernel(x, norm_g, ffn1_gu, ffn1_down, ffn2_gu, ffn2_down, ev_w_in, ev_w_out, ev_a_ln_g, ev_a_ws, ev_a_bs, ev_b_conv, ev_b_a_log, ev_b_dt_bias, ev_b_onorm_g, od_w_in, od_w_out, od_b_i, od_b_f, od_onorm_g):
    b, s, d = x.shape
    x2 = x.reshape(b * s, d)
    depth = norm_g.shape[0]
    for layer in range(depth):
        ng = norm_g[layer]
        j = layer // 2
        x2 = _ffn(x2, ng[0], _to_bf16(ffn1_gu, layer), _to_bf16(ffn1_down, layer), ng[1])
        x3 = x2.reshape(b, s, d)
        if layer % 2 == 0:
            ya, q, k, v, gate, colp, rowp = _even_pre(
                x3, ng[2], ev_w_in[j], ev_a_ln_g[j], ev_a_ws[j], ev_a_bs[j], ev_b_conv[j],
                ev_b_a_log[j], ev_b_dt_bias[j])
            u, w, qg, attn, kdt, dec = _gdn_wy(q, k, v, colp, rowp)
            yb = _gdn_scan(u, w, qg, attn, kdt, dec, gate, ev_b_onorm_g[j])
            mix = (ya.reshape(b * s, -1), 0, yb.reshape(b * s, -1), 0, ev_w_out[j])
        else:
            q, k, v, og, colp, rowp = _odd_pre(x3, ng[2], od_w_in[j], od_b_i[j], od_b_f[j])
            pk, qs, en, se = _mlstm_local(q, k, colp, rowp)
            y = _mlstm_scan(pk, qs, en, se, v, og, od_onorm_g[j]).reshape(b * s, -1)
            mix = (y, 0, y, 1, od_w_out[j])
        x2 = _post_ffn(*mix, ng[3], x2, ng[4], _to_bf16(ffn2_gu, layer), _to_bf16(ffn2_down, layer), ng[5])
    return x2.reshape(b, s, d)
```
